```python
import math
import jax, jax.numpy as jnp
from jax import lax
import numpy as np

D_MODEL = 2048
BATCH = 2
SEQ = 4096
DEPTH = 1

HEAD_DIM = 128
N_DIFF_HEADS = 8
N_NA_HEADS = 8
DIFF_HALF = HEAD_DIM // 2
DIFF_WIDTH = N_DIFF_HEADS * HEAD_DIM
NA_WIDTH = N_NA_HEADS * HEAD_DIM
MIX_WIDTH = DIFF_WIDTH + NA_WIDTH
IN_COLS = 3 * DIFF_WIDTH + 3 * NA_WIDTH
ROT_DIM = DIFF_HALF // 4
ROPE_THETA = 500000.0
GRID_W = 64
WIN_H = 8
WIN_W = 16
D_FF = 4 * D_MODEL
Q_BLOCK = 128
EPS = 1e-5
NEG_INF = -1e30

kernel_name = "hymba_diffattn_natten_sqrelu_encoder"


def rmsnorm(x, g):
    xf = x.astype(jnp.float32)
    y = xf * lax.rsqrt(jnp.mean(xf * xf, axis=-1, keepdims=True) + EPS)
    return (y * g.astype(jnp.float32)).astype(x.dtype)


def rotary_tables(seq, dtype):
    inv_freq = jnp.power(ROPE_THETA, -jnp.arange(0, ROT_DIM, 2, dtype=jnp.float32) / ROT_DIM)
    ang = jnp.arange(seq, dtype=jnp.float32)[:, None] * inv_freq[None, :]
    ang = jnp.concatenate([ang, ang], axis=-1)
    return jnp.cos(ang).astype(dtype), jnp.sin(ang).astype(dtype)


def partial_rotary(t, cos, sin):
    rot, rest = t[..., :ROT_DIM], t[..., ROT_DIM:]
    x1, x2 = rot[..., : ROT_DIM // 2], rot[..., ROT_DIM // 2:]
    rotated = jnp.concatenate([-x2, x1], axis=-1)
    c = cos[None, :, None, None, :]
    s = sin[None, :, None, None, :]
    return jnp.concatenate([rot * c + rotated * s, rest], axis=-1)


def diff_attention(q, k, v, lam, lambda_init, subln_g):
    B, S, H = q.shape[0], q.shape[1], q.shape[2]
    nqb = S // Q_BLOCK
    scale = 1.0 / math.sqrt(DIFF_HALF)
    k1, k2 = k[:, :, :, 0], k[:, :, :, 1]

    def to_blocks(t):
        return t.reshape(B, nqb, Q_BLOCK, H, DIFF_HALF).transpose(1, 0, 3, 2, 4)

    q1b, q2b = to_blocks(q[:, :, :, 0]), to_blocks(q[:, :, :, 1])

    def block(args):
        qa, qb = args
        s1 = jnp.einsum('bhqd,bkhd->bhqk', qa, k1).astype(jnp.float32) * scale
        s2 = jnp.einsum('bhqd,bkhd->bhqk', qb, k2).astype(jnp.float32) * scale
        a = jax.nn.softmax(s1, axis=-1) - lam * jax.nn.softmax(s2, axis=-1)
        return jnp.einsum('bhqk,bkhd->bqhd', a.astype(v.dtype), v)

    out = lax.map(block, (q1b, q2b))
    out = out.transpose(1, 0, 2, 3, 4).reshape(B, S, H, HEAD_DIM)
    out = rmsnorm(out, subln_g) * (1.0 - lambda_init)
    return out


def neighbourhood_attention(q, k, v, rel_bias):
    B, S, H, Dh = q.shape
    rows = S // GRID_W
    kh = min(WIN_H, rows)
    kw = WIN_W
    scale = 1.0 / math.sqrt(Dh)
    qg = q.reshape(B, rows, GRID_W, H, Dh)
    kg = k.reshape(B, rows, GRID_W, H, Dh)
    vg = v.reshape(B, rows, GRID_W, H, Dh)

    r = jnp.arange(rows)
    row_start = jnp.clip(r - kh // 2, 0, rows - kh)
    row_idx = row_start[:, None] + jnp.arange(kh)[None, :]
    c = jnp.arange(GRID_W)
    col_start = jnp.clip(c - kw // 2, 0, GRID_W - kw)
    kc = jnp.arange(GRID_W)
    col_in = (kc[None, :] >= col_start[:, None]) & (kc[None, :] < col_start[:, None] + kw)

    k_band = kg[:, row_idx]
    v_band = vg[:, row_idx]

    scores = jnp.einsum('brchd,brjkhd->bhrcjk', qg, k_band).astype(jnp.float32) * scale

    ri = row_idx - r[:, None] + (WIN_H - 1)
    ci = jnp.clip(kc[None, :] - c[:, None] + (WIN_W - 1), 0, 2 * WIN_W - 2)
    bias = rel_bias.astype(jnp.float32)[:, ri[:, None, :, None], ci[None, :, None, :]]

    logits = scores + bias[None]
    logits = jnp.where(col_in[None, None, None, :, None, :], logits, NEG_INF)
    probs = jax.nn.softmax(logits, axis=(-2, -1))
    out = jnp.einsum('bhrcjk,brjkhd->brchd', probs.astype(v.dtype), v_band)
    return out.reshape(B, S, H, Dh)


def setup_inputs(seed: int = 0) -> dict:
    key = jax.random.key(seed)
    ks = jax.random.split(key, 16)
    f32 = jnp.float32
    x = jax.random.normal(ks[0], (BATCH, SEQ, D_MODEL), f32)
    norm_mix_g = 1.0 + 0.02 * jax.random.normal(ks[1], (DEPTH, D_MODEL), f32)
    w_in = jax.random.normal(ks[2], (DEPTH, D_MODEL, IN_COLS), f32) * D_MODEL ** -0.5
    lambda_q1 = 0.1 * jax.random.normal(ks[3], (DEPTH, DIFF_HALF), f32)
    lambda_k1 = 0.1 * jax.random.normal(ks[4], (DEPTH, DIFF_HALF), f32)
    lambda_q2 = 0.1 * jax.random.normal(ks[5], (DEPTH, DIFF_HALF), f32)
    lambda_k2 = 0.1 * jax.random.normal(ks[6], (DEPTH, DIFF_HALF), f32)
    diff_subln_g = 1.0 + 0.02 * jax.random.normal(ks[7], (DEPTH, HEAD_DIM), f32)
    na_rel_bias = 0.02 * jax.random.normal(ks[8], (DEPTH, N_NA_HEADS, 2 * WIN_H - 1, 2 * WIN_W - 1), f32)
    w_out = jax.random.normal(ks[9], (DEPTH, MIX_WIDTH, D_MODEL), f32) * MIX_WIDTH ** -0.5
    norm_mlp_g = 1.0 + 0.02 * jax.random.normal(ks[10], (DEPTH, D_MODEL), f32)
    w_up = jax.random.normal(ks[11], (DEPTH, D_MODEL, D_FF), f32) * D_MODEL ** -0.5
    w_down = jax.random.normal(ks[12], (DEPTH, D_FF, D_MODEL), f32) * D_FF ** -0.5
    norm_final_g = 1.0 + 0.02 * jax.random.normal(ks[13], (D_MODEL,), f32)
    return {"x": x, "norm_mix_g": norm_mix_g, "w_in": w_in,
            "lambda_q1": lambda_q1, "lambda_k1": lambda_k1,
            "lambda_q2": lambda_q2, "lambda_k2": lambda_k2,
            "diff_subln_g": diff_subln_g, "na_rel_bias": na_rel_bias,
            "w_out": w_out, "norm_mlp_g": norm_mlp_g, "w_up": w_up,
            "w_down": w_down, "norm_final_g": norm_final_g}


def reference(x, norm_mix_g, w_in, lambda_q1, lambda_k1, lambda_q2, lambda_k2,
              diff_subln_g, na_rel_bias, w_out, norm_mlp_g, w_up, w_down, norm_final_g):
    B, S, _ = x.shape
    cos, sin = rotary_tables(S, x.dtype)
    for layer in range(DEPTH):
        lambda_init = 0.8 - 0.6 * math.exp(-0.3 * layer)
        h = rmsnorm(x, norm_mix_g[layer])
        proj = h @ w_in[layer]
        dq, dk, dv, nq, nk, nv = jnp.split(proj, 6, axis=-1)

        dq = partial_rotary(dq.reshape(B, S, N_DIFF_HEADS, 2, DIFF_HALF), cos, sin)
        dk = partial_rotary(dk.reshape(B, S, N_DIFF_HEADS, 2, DIFF_HALF), cos, sin)
        dv = dv.reshape(B, S, N_DIFF_HEADS, HEAD_DIM)
        lam = (jnp.exp(jnp.sum(lambda_q1[layer].astype(jnp.float32) * lambda_k1[layer].astype(jnp.float32)))
               - jnp.exp(jnp.sum(lambda_q2[layer].astype(jnp.float32) * lambda_k2[layer].astype(jnp.float32)))
               + lambda_init)
        a_out = diff_attention(dq, dk, dv, lam, lambda_init, diff_subln_g[layer])

        n_out = neighbourhood_attention(nq.reshape(B, S, N_NA_HEADS, HEAD_DIM),
                                        nk.reshape(B, S, N_NA_HEADS, HEAD_DIM),
                                        nv.reshape(B, S, N_NA_HEADS, HEAD_DIM),
                                        na_rel_bias[layer])

        mix = jnp.concatenate([a_out.reshape(B, S, DIFF_WIDTH), n_out.reshape(B, S, NA_WIDTH)], axis=-1)
        x = x + mix @ w_out[layer]

        u = rmsnorm(x, norm_mlp_g[layer]) @ w_up[layer]
        x = x + jnp.square(jax.nn.relu(u)) @ w_down[layer]
    return rmsnorm(x, norm_final_g)
```

```python
import functools
import math

import jax
import jax.numpy as jnp
from jax import lax
from jax.experimental import pallas as pl
from jax.experimental.pallas import tpu as pltpu

F32 = jnp.float32
BF16 = jnp.bfloat16

D_MODEL = 2048
HEAD_DIM = 128
N_HEADS = 8
GROUP_WIDTH = N_HEADS * HEAD_DIM
N_GROUPS = 6
DIFF_HALF = HEAD_DIM // 2
ROT_DIM = DIFF_HALF // 4
ROT_HALF = ROT_DIM // 2
ROPE_THETA = 500000.0
GRID_W = 64
WIN_H = 8
WIN_W = 16
D_FF = 4 * D_MODEL
EPS = 1e-5
NEG_INF = -1e30
LAMBDA_INIT = 0.8 - 0.6 * math.exp(-0.3 * 0)

LANES = 128
SUBLANES = 8
VMEM_LIMIT_BYTES = 56 * 1024 * 1024

PROJ_TM = 1024
ATTN_TQ = 256
ATTN_KC = 512
OUT_TM = 512
MLP_TM = 512
MLP_TF = 1024
NA_ROWS_PER_ITER = 4


def _compiler_params(semantics):
    return pltpu.CompilerParams(dimension_semantics=semantics, vmem_limit_bytes=VMEM_LIMIT_BYTES)


def _rms_scale(x):
    return lax.rsqrt(jnp.mean(x * x, axis=-1, keepdims=True) + EPS)


def _in_proj_kernel(x_ref, g_ref, w_ref, rc_ref, rs_lo_ref, rs_hi_ref, o_ref, h_ref):
    j = pl.program_id(1)

    @pl.when(j == 0)
    def _():
        x = x_ref[...]
        h_ref[...] = (x * _rms_scale(x) * g_ref[...]).astype(BF16)

    acc = jnp.dot(h_ref[...], w_ref[...], preferred_element_type=F32)

    @pl.when(j < 2)
    def _():
        q_scale = jnp.where(j == 0, 1.0 / math.sqrt(DIFF_HALF), 1.0).astype(F32)
        c = rc_ref[...]
        s_lo = rs_lo_ref[...]
        s_hi = rs_hi_ref[...]
        for hh in range(N_HEADS):
            a = acc[:, hh * HEAD_DIM:(hh + 1) * HEAD_DIM]
            r = a * c + pltpu.roll(a, LANES - ROT_HALF, 1) * s_lo + pltpu.roll(a, ROT_HALF, 1) * s_hi
            o_ref[:, hh * HEAD_DIM:(hh + 1) * HEAD_DIM] = (r * q_scale).astype(BF16)

    @pl.when(j >= 2)
    def _():
        o_ref[...] = acc.astype(BF16)


def _rotary_lane_tables(seq):
    inv_freq = jnp.power(ROPE_THETA, -jnp.arange(0, ROT_DIM, 2, dtype=F32) / ROT_DIM)
    ang = jnp.arange(seq, dtype=F32)[:, None] * inv_freq[None, :]
    cos, sin = jnp.cos(ang), jnp.sin(ang)
    one = jnp.ones((seq, DIFF_HALF - ROT_DIM), F32)
    zero8 = jnp.zeros((seq, ROT_HALF), F32)
    zero_rest = jnp.zeros((seq, DIFF_HALF - ROT_DIM), F32)
    c_half = jnp.concatenate([cos, cos, one], axis=-1)
    lo_half = jnp.concatenate([-sin, zero8, zero_rest], axis=-1)
    hi_half = jnp.concatenate([zero8, sin, zero_rest], axis=-1)
    tile2 = lambda t: jnp.concatenate([t, t], axis=-1)
    return tile2(c_half), tile2(lo_half), tile2(hi_half)


def _in_proj(x2, g, w_bf16, seq):
    rows = x2.shape[0]
    tm = PROJ_TM
    rc, rs_lo, rs_hi = _rotary_lane_tables(seq)
    pos_blocks = seq // tm
    tab_spec = pl.BlockSpec((tm, LANES), lambda i, j: (i % pos_blocks, 0))
    return pl.pallas_call(
        _in_proj_kernel,
        grid=(rows // tm, N_GROUPS),
        in_specs=[
            pl.BlockSpec((tm, D_MODEL), lambda i, j: (i, 0)),
            pl.BlockSpec((1, D_MODEL), lambda i, j: (0, 0)),
            pl.BlockSpec((D_MODEL, GROUP_WIDTH), lambda i, j: (0, j)),
            tab_spec, tab_spec, tab_spec,
        ],
        out_specs=pl.BlockSpec((tm, GROUP_WIDTH), lambda i, j: (i, j)),
        out_shape=jax.ShapeDtypeStruct((rows, N_GROUPS * GROUP_WIDTH), BF16),
        scratch_shapes=[pltpu.VMEM((tm, D_MODEL), BF16)],
        compiler_params=_compiler_params(("arbitrary", "arbitrary")),
        name="in_proj",
    )(x2, g, w_bf16, rc, rs_lo, rs_hi)


def _diff_attn_kernel(lq1_ref, lk1_ref, lq2_ref, lk2_ref, g_ref, q_ref, k_ref, v_ref, o_ref, vt_ref, s_ref):
    seq = k_ref.shape[0]
    tq = q_ref.shape[0]
    n_chunks = seq // ATTN_KC
    fold = ATTN_KC // SUBLANES

    @pl.when(pl.program_id(2) == 0)
    def _():
        vt_ref[...] = v_ref[...].T

    q = q_ref[...]
    lane = lax.broadcasted_iota(jnp.int32, q.shape, 1)
    zero = jnp.zeros_like(q)
    q_parts = (jnp.where(lane < DIFF_HALF, q, zero), jnp.where(lane >= DIFF_HALF, q, zero))

    outs = []
    for comp in range(2):
        m8 = None
        for c in range(n_chunks):
            ks = k_ref[c * ATTN_KC:(c + 1) * ATTN_KC, :]
            s = lax.dot_general(ks, q_parts[comp], (((1,), (1,)), ((), ())), preferred_element_type=F32)
            s_ref[comp, c * ATTN_KC:(c + 1) * ATTN_KC, :] = s
            cm = jnp.max(s.reshape(fold, SUBLANES, tq), axis=0)
            m8 = cm if m8 is None else jnp.maximum(m8, cm)
        m = jnp.max(m8, axis=0, keepdims=True)
        l8 = jnp.zeros((SUBLANES, tq), F32)
        o = jnp.zeros((HEAD_DIM, tq), F32)
        for c in range(n_chunks):
            p = jnp.exp(s_ref[comp, c * ATTN_KC:(c + 1) * ATTN_KC, :] - m)
            l8 = l8 + jnp.sum(p.reshape(fold, SUBLANES, tq), axis=0)
            o = o + jnp.dot(vt_ref[:, c * ATTN_KC:(c + 1) * ATTN_KC], p.astype(BF16),
                            preferred_element_type=F32)
        outs.append(o / jnp.sum(l8, axis=0, keepdims=True))

    lam = (jnp.exp(jnp.sum(lq1_ref[...] * lk1_ref[...])) - jnp.exp(jnp.sum(lq2_ref[...] * lk2_ref[...]))
           + LAMBDA_INIT)
    out = (outs[0] - lam * outs[1]).T
    y = out * _rms_scale(out) * g_ref[...]
    o_ref[...] = (y * (1.0 - LAMBDA_INIT)).astype(o_ref.dtype)


def _diff_attn(proj, lq1, lk1, lq2, lk2, subln_g, batch, seq):
    tq = ATTN_TQ
    qt = seq // tq
    lam_spec = pl.BlockSpec((1, DIFF_HALF), lambda b, h, i: (0, 0))
    return pl.pallas_call(
        _diff_attn_kernel,
        grid=(batch, N_HEADS, qt),
        in_specs=[
            lam_spec, lam_spec, lam_spec, lam_spec,
            pl.BlockSpec((1, HEAD_DIM), lambda b, h, i: (0, 0)),
            pl.BlockSpec((tq, HEAD_DIM), lambda b, h, i: (b * qt + i, h)),
            pl.BlockSpec((seq, HEAD_DIM), lambda b, h, i: (b, N_HEADS + h)),
            pl.BlockSpec((seq, HEAD_DIM), lambda b, h, i: (b, 2 * N_HEADS + h)),
        ],
        out_specs=pl.BlockSpec((tq, HEAD_DIM), lambda b, h, i: (b * qt + i, h)),
        out_shape=jax.ShapeDtypeStruct((batch * seq, GROUP_WIDTH), BF16),
        scratch_shapes=[pltpu.VMEM((HEAD_DIM, seq), BF16), pltpu.VMEM((2, seq, tq), F32)],
        compiler_params=_compiler_params(("arbitrary", "arbitrary", "arbitrary")),
        name="diff_attn",
    )(lq1, lk1, lq2, lk2, subln_g, proj, proj, proj)


N_REL_ROWS = 2 * WIN_H - 1
N_REL_COLS = 2 * WIN_W - 1
N_PAIR_TABLES = N_REL_ROWS - 1


def _na_kernel(rb_ref, q_ref, k_ref, v_ref, o_ref, tab_ref):
    head = pl.program_id(0)
    rows = q_ref.shape[0] // GRID_W
    band = WIN_H * GRID_W
    scale = 1.0 / math.sqrt(HEAD_DIM)

    @pl.when(pl.program_id(1) == 0)
    def _():
        qc = lax.broadcasted_iota(jnp.int32, (GRID_W, LANES), 0)
        ln = lax.broadcasted_iota(jnp.int32, (GRID_W, LANES), 1)
        kc = ln & (GRID_W - 1)
        upper = ln >= GRID_W
        rel = jnp.clip(kc - qc + (WIN_W - 1), 0, N_REL_COLS - 1)
        col_start = jnp.clip(qc - WIN_W // 2, 0, GRID_W - WIN_W)
        inside = (kc >= col_start) & (kc < col_start + WIN_W)
        base = head * (N_REL_ROWS * N_REL_COLS)

        def build(t, carry):
            acc = jnp.zeros((GRID_W, LANES), F32)
            for d in range(N_REL_COLS):
                lo = rb_ref[base + t * N_REL_COLS + d]
                hi = rb_ref[base + (t + 1) * N_REL_COLS + d]
                acc = jnp.where(rel == d, jnp.where(upper, hi, lo), acc)
            tab_ref[t] = jnp.where(inside, acc, NEG_INF)
            return carry

        lax.fori_loop(0, N_PAIR_TABLES, build, 0)

    def one_row(r):
        r_start = jnp.clip(r - WIN_H // 2, 0, rows - WIN_H)
        rel0 = r_start - r + (WIN_H - 1)
        q = q_ref[pl.ds(pl.multiple_of(r * GRID_W, GRID_W), GRID_W), :]
        kb = k_ref[pl.ds(pl.multiple_of(r_start * GRID_W, GRID_W), band), :]
        vb = v_ref[pl.ds(pl.multiple_of(r_start * GRID_W, GRID_W), band), :]
        s = lax.dot_general(q, kb, (((1,), (1,)), ((), ())), preferred_element_type=F32)
        parts = [s[:, jj * LANES:(jj + 1) * LANES] * scale + tab_ref[rel0 + 2 * jj] for jj in range(band // LANES)]
        logits = jnp.concatenate(parts, axis=-1)
        m = jnp.max(logits, axis=-1, keepdims=True)
        p = jnp.exp(logits - m)
        l = jnp.sum(p, axis=-1, keepdims=True)
        o = jnp.dot(p.astype(BF16), vb, preferred_element_type=F32) / l
        o_ref[pl.ds(pl.multiple_of(r * GRID_W, GRID_W), GRID_W), :] = o.astype(o_ref.dtype)

    def body(it, carry):
        for u in range(NA_ROWS_PER_ITER):
            one_row(it * NA_ROWS_PER_ITER + u)
        return carry

    lax.fori_loop(0, rows // NA_ROWS_PER_ITER, body, 0)


def _na_attn(proj, rel_bias_flat, batch, seq):
    blk = lambda col0: pl.BlockSpec((seq, HEAD_DIM), lambda h, b: (b, col0 + h))
    return pl.pallas_call(
        _na_kernel,
        grid=(N_HEADS, batch),
        in_specs=[
            pl.BlockSpec(memory_space=pltpu.SMEM),
            blk(3 * N_HEADS), blk(4 * N_HEADS), blk(5 * N_HEADS),
        ],
        out_specs=pl.BlockSpec((seq, HEAD_DIM), lambda h, b: (b, h)),
        out_shape=jax.ShapeDtypeStruct((batch * seq, GROUP_WIDTH), BF16),
        scratch_shapes=[pltpu.VMEM((N_PAIR_TABLES, GRID_W, LANES), F32)],
        compiler_params=_compiler_params(("arbitrary", "arbitrary")),
        name="na_attn",
    )(rel_bias_flat, proj, proj, proj)


def _out_proj_kernel(x_ref, a_ref, n_ref, wa_ref, wn_ref, o_ref):
    acc = jnp.dot(a_ref[...], wa_ref[...], preferred_element_type=F32)
    acc = acc + jnp.dot(n_ref[...], wn_ref[...], preferred_element_type=F32)
    o_ref[...] = x_ref[...] + acc


def _out_proj(x2, a_out, n_out, w_out_bf16):
    rows = x2.shape[0]
    tm = OUT_TM
    return pl.pallas_call(
        _out_proj_kernel,
        grid=(rows // tm,),
        in_specs=[
            pl.BlockSpec((tm, D_MODEL), lambda i: (i, 0)),
            pl.BlockSpec((tm, GROUP_WIDTH), lambda i: (i, 0)),
            pl.BlockSpec((tm, GROUP_WIDTH), lambda i: (i, 0)),
            pl.BlockSpec((GROUP_WIDTH, D_MODEL), lambda i: (0, 0)),
            pl.BlockSpec((GROUP_WIDTH, D_MODEL), lambda i: (1, 0)),
        ],
        out_specs=pl.BlockSpec((tm, D_MODEL), lambda i: (i, 0)),
        out_shape=jax.ShapeDtypeStruct((rows, D_MODEL), F32),
        compiler_params=_compiler_params(("arbitrary",)),
        name="out_proj",
    )(x2, a_out, n_out, w_out_bf16, w_out_bf16)


def _mlp_kernel(x_ref, g_ref, wu_ref, wd_ref, gf_ref, o_ref, h_ref):
    j = pl.program_id(1)

    @pl.when(j == 0)
    def _():
        x = x_ref[...]
        h_ref[...] = (x * _rms_scale(x) * g_ref[...]).astype(BF16)
        o_ref[...] = x

    u = jnp.dot(h_ref[...], wu_ref[...], preferred_element_type=F32)
    r = jnp.maximum(u, 0.0)
    o_ref[...] += jnp.dot((r * r).astype(BF16), wd_ref[...], preferred_element_type=F32)

    @pl.when(j == pl.num_programs(1) - 1)
    def _():
        y = o_ref[...]
        o_ref[...] = y * _rms_scale(y) * gf_ref[...]


def _mlp(x1, g_mlp, w_up_bf16, w_down_bf16, g_final):
    rows = x1.shape[0]
    tm, tf = MLP_TM, MLP_TF
    return pl.pallas_call(
        _mlp_kernel,
        grid=(rows // tm, D_FF // tf),
        in_specs=[
            pl.BlockSpec((tm, D_MODEL), lambda i, j: (i, 0)),
            pl.BlockSpec((1, D_MODEL), lambda i, j: (0, 0)),
            pl.BlockSpec((D_MODEL, tf), lambda i, j: (0, j)),
            pl.BlockSpec((tf, D_MODEL), lambda i, j: (j, 0)),
            pl.BlockSpec((1, D_MODEL), lambda i, j: (0, 0)),
        ],
        out_specs=pl.BlockSpec((tm, D_MODEL), lambda i, j: (i, 0)),
        out_shape=jax.ShapeDtypeStruct((rows, D_MODEL), F32),
        scratch_shapes=[pltpu.VMEM((tm, D_MODEL), BF16)],
        compiler_params=_compiler_params(("arbitrary", "arbitrary")),
        name="mlp",
    )(x1, g_mlp, w_up_bf16, w_down_bf16, g_final)


def kernel(x, norm_mix_g, w_in, lambda_q1, lambda_k1, lambda_q2, lambda_k2, diff_subln_g, na_rel_bias, w_out,
           norm_mlp_g, w_up, w_down, norm_final_g):
    batch, seq, d_model = x.shape
    assert d_model == D_MODEL and w_in.shape == (1, D_MODEL, N_GROUPS * GROUP_WIDTH)
    assert seq % PROJ_TM == 0 and seq % ATTN_TQ == 0 and seq % ATTN_KC == 0 and seq % (GRID_W * WIN_H) == 0
    x2 = x.reshape(batch * seq, D_MODEL)

    proj = _in_proj(x2, norm_mix_g, w_in[0].astype(BF16), seq)
    a_out = _diff_attn(proj, lambda_q1, lambda_k1, lambda_q2, lambda_k2, diff_subln_g, batch, seq)
    n_out = _na_attn(proj, na_rel_bias[0].reshape(-1), batch, seq)
    x1 = _out_proj(x2, a_out, n_out, w_out[0].astype(BF16))
    y = _mlp(x1, norm_mlp_g, w_up[0].astype(BF16), w_down[0].astype(BF16), norm_final_g.reshape(1, D_MODEL))
    return y.reshape(batch, seq, D_MODEL)
```

```python
import functools
import math

import jax
import jax.numpy as jnp
from jax import lax
from jax.experimental import pallas as pl
from jax.experimental.pallas import tpu as pltpu

F32 = jnp.float32
BF16 = jnp.bfloat16

D_MODEL = 2048
HEAD_DIM = 128
N_HEADS = 8
GROUP_WIDTH = N_HEADS * HEAD_DIM
N_GROUPS = 6
DIFF_HALF = HEAD_DIM // 2
ROT_DIM = DIFF_HALF // 4
ROT_HALF = ROT_DIM // 2
ROPE_THETA = 500000.0
GRID_W = 64
WIN_H = 8
WIN_W = 16
D_FF = 4 * D_MODEL
EPS = 1e-5
NEG_INF = -1e30
LAMBDA_INIT = 0.8 - 0.6 * math.exp(-0.3 * 0)

LANES = 128
SUBLANES = 8
VMEM_LIMIT_BYTES = 56 * 1024 * 1024

PROJ_TM = 1024
ATTN_TQ = 256
ATTN_KC = 512
OUT_TM = 512
MLP_TM = 512
MLP_TF = 1024
NA_ROWS_PER_ITER = 16


def _compiler_params(semantics):
    return pltpu.CompilerParams(dimension_semantics=semantics, vmem_limit_bytes=VMEM_LIMIT_BYTES)


def _rms_scale(x):
    return lax.rsqrt(jnp.mean(x * x, axis=-1, keepdims=True) + EPS)


def _in_proj_kernel(x_ref, g_ref, w_ref, rc_ref, rs_lo_ref, rs_hi_ref, o_ref, h_ref):
    j = pl.program_id(1)

    @pl.when(j == 0)
    def _():
        x = x_ref[...]
        h_ref[...] = (x * _rms_scale(x) * g_ref[...]).astype(BF16)

    acc = jnp.dot(h_ref[...], w_ref[...], preferred_element_type=F32)

    @pl.when(j < 2)
    def _():
        q_scale = jnp.where(j == 0, 1.0 / math.sqrt(DIFF_HALF), 1.0).astype(F32)
        c = rc_ref[...]
        s_lo = rs_lo_ref[...]
        s_hi = rs_hi_ref[...]
        for hh in range(N_HEADS):
            a = acc[:, hh * HEAD_DIM:(hh + 1) * HEAD_DIM]
            r = a * c + pltpu.roll(a, LANES - ROT_HALF, 1) * s_lo + pltpu.roll(a, ROT_HALF, 1) * s_hi
            o_ref[:, hh * HEAD_DIM:(hh + 1) * HEAD_DIM] = (r * q_scale).astype(BF16)

    @pl.when(j >= 2)
    def _():
        o_ref[...] = acc.astype(BF16)


def _rotary_lane_tables(seq):
    inv_freq = jnp.power(ROPE_THETA, -jnp.arange(0, ROT_DIM, 2, dtype=F32) / ROT_DIM)
    ang = jnp.arange(seq, dtype=F32)[:, None] * inv_freq[None, :]
    cos, sin = jnp.cos(ang), jnp.sin(ang)
    one = jnp.ones((seq, DIFF_HALF - ROT_DIM), F32)
    zero8 = jnp.zeros((seq, ROT_HALF), F32)
    zero_rest = jnp.zeros((seq, DIFF_HALF - ROT_DIM), F32)
    c_half = jnp.concatenate([cos, cos, one], axis=-1)
    lo_half = jnp.concatenate([-sin, zero8, zero_rest], axis=-1)
    hi_half = jnp.concatenate([zero8, sin, zero_rest], axis=-1)
    tile2 = lambda t: jnp.concatenate([t, t], axis=-1)
    return tile2(c_half), tile2(lo_half), tile2(hi_half)


def _in_proj(x2, g, w_bf16, seq):
    rows = x2.shape[0]
    tm = PROJ_TM
    rc, rs_lo, rs_hi = _rotary_lane_tables(seq)
    pos_blocks = seq // tm
    tab_spec = pl.BlockSpec((tm, LANES), lambda i, j: (i % pos_blocks, 0))
    return pl.pallas_call(
        _in_proj_kernel,
        grid=(rows // tm, N_GROUPS),
        in_specs=[
            pl.BlockSpec((tm, D_MODEL), lambda i, j: (i, 0)),
            pl.BlockSpec((1, D_MODEL), lambda i, j: (0, 0)),
            pl.BlockSpec((D_MODEL, GROUP_WIDTH), lambda i, j: (0, j)),
            tab_spec, tab_spec, tab_spec,
        ],
        out_specs=pl.BlockSpec((tm, GROUP_WIDTH), lambda i, j: (i, j)),
        out_shape=jax.ShapeDtypeStruct((rows, N_GROUPS * GROUP_WIDTH), BF16),
        scratch_shapes=[pltpu.VMEM((tm, D_MODEL), BF16)],
        compiler_params=_compiler_params(("arbitrary", "arbitrary")),
        name="in_proj",
    )(x2, g, w_bf16, rc, rs_lo, rs_hi)


def _diff_attn_kernel(lq1_ref, lk1_ref, lq2_ref, lk2_ref, g_ref, q_ref, k_ref, v_ref, o_ref, vt_ref, s_ref):
    seq = k_ref.shape[0]
    tq = ATTN_TQ
    n_tiles = seq // tq
    n_chunks = seq // ATTN_KC
    fold = ATTN_KC // SUBLANES
    assert n_tiles % 2 == 0 and n_tiles >= 4

    vt_ref[...] = v_ref[...].T
    lam = (jnp.exp(jnp.sum(lq1_ref[...] * lk1_ref[...])) - jnp.exp(jnp.sum(lq2_ref[...] * lk2_ref[...]))
           + LAMBDA_INIT)
    lane = lax.broadcasted_iota(jnp.int32, (tq, HEAD_DIM), 1)

    def load_q_parts(t):
        q = q_ref[pl.ds(pl.multiple_of(t * tq, tq), tq), :]
        zero = jnp.zeros_like(q)
        return jnp.where(lane < DIFF_HALF, q, zero), jnp.where(lane >= DIFF_HALF, q, zero)

    def score_chunk(slot, c, q_parts, m8):
        ks = k_ref[c * ATTN_KC:(c + 1) * ATTN_KC, :]
        new_m8 = []
        for comp in range(2):
            s = lax.dot_general(ks, q_parts[comp], (((1,), (1,)), ((), ())), preferred_element_type=F32)
            s_ref[slot, comp, c * ATTN_KC:(c + 1) * ATTN_KC, :] = s
            cm = jnp.max(s.reshape(fold, SUBLANES, tq), axis=0)
            new_m8.append(cm if m8 is None else jnp.maximum(m8[comp], cm))
        return new_m8

    def prob_chunk(slot, c, m, acc):
        vts = vt_ref[:, c * ATTN_KC:(c + 1) * ATTN_KC]
        new_acc = []
        for comp in range(2):
            l8, o = acc[comp]
            p = jnp.exp(s_ref[slot, comp, c * ATTN_KC:(c + 1) * ATTN_KC, :] - m[comp])
            l8 = l8 + jnp.sum(p.reshape(fold, SUBLANES, tq), axis=0)
            o = o + jnp.dot(vts, p.astype(BF16), preferred_element_type=F32)
            new_acc.append((l8, o))
        return new_acc

    def col_max(m8):
        return [jnp.max(x, axis=0, keepdims=True) for x in m8]

    def zero_acc():
        return [(jnp.zeros((SUBLANES, tq), F32), jnp.zeros((HEAD_DIM, tq), F32)) for _ in range(2)]

    def finish(t, acc):
        outs = [o / jnp.sum(l8, axis=0, keepdims=True) for l8, o in acc]
        out = (outs[0] - lam * outs[1]).T
        y = out * _rms_scale(out) * g_ref[...]
        o_ref[pl.ds(pl.multiple_of(t * tq, tq), tq), :] = (y * (1.0 - LAMBDA_INIT)).astype(o_ref.dtype)

    def stage(t_next, slot_next, t_cur, slot_cur, m_cur):
        q_parts = load_q_parts(t_next) if t_next is not None else None
        m8, acc = None, zero_acc()
        for c in range(n_chunks):
            if q_parts is not None:
                m8 = score_chunk(slot_next, c, q_parts, m8)
            if t_cur is not None:
                acc = prob_chunk(slot_cur, c, m_cur, acc)
        if t_cur is not None:
            finish(t_cur, acc)
        return col_max(m8) if m8 is not None else None

    m0 = stage(0, 0, None, None, None)

    def pair(u, m_even):
        m_odd = stage(2 * u + 1, 1, 2 * u, 0, m_even)
        return tuple(stage(2 * u + 2, 0, 2 * u + 1, 1, m_odd))

    m_even = lax.fori_loop(0, n_tiles // 2 - 1, pair, tuple(m0))
    m_odd = stage(n_tiles - 1, 1, n_tiles - 2, 0, m_even)
    stage(None, None, n_tiles - 1, 1, m_odd)


def _diff_attn(proj, lq1, lk1, lq2, lk2, subln_g, batch, seq):
    lam_spec = pl.BlockSpec((1, DIFF_HALF), lambda b, h: (0, 0))
    head_blk = lambda col0: pl.BlockSpec((seq, HEAD_DIM), lambda b, h: (b, col0 + h))
    return pl.pallas_call(
        _diff_attn_kernel,
        grid=(batch, N_HEADS),
        in_specs=[
            lam_spec, lam_spec, lam_spec, lam_spec,
            pl.BlockSpec((1, HEAD_DIM), lambda b, h: (0, 0)),
            head_blk(0), head_blk(N_HEADS), head_blk(2 * N_HEADS),
        ],
        out_specs=pl.BlockSpec((seq, HEAD_DIM), lambda b, h: (b, h)),
        out_shape=jax.ShapeDtypeStruct((batch * seq, GROUP_WIDTH), BF16),
        scratch_shapes=[pltpu.VMEM((HEAD_DIM, seq), BF16), pltpu.VMEM((2, 2, seq, ATTN_TQ), F32)],
        compiler_params=_compiler_params(("arbitrary", "arbitrary")),
        name="diff_attn",
    )(lq1, lk1, lq2, lk2, subln_g, proj, proj, proj)


N_REL_ROWS = 2 * WIN_H - 1
N_REL_COLS = 2 * WIN_W - 1
N_PAIR_TABLES = N_REL_ROWS - 1


def _na_kernel(rb_ref, q_ref, k_ref, v_ref, o_ref, tab_ref):
    head = pl.program_id(0)
    rows = q_ref.shape[0] // GRID_W
    band = WIN_H * GRID_W
    scale = 1.0 / math.sqrt(HEAD_DIM)

    @pl.when(pl.program_id(1) == 0)
    def _():
        qc = lax.broadcasted_iota(jnp.int32, (GRID_W, LANES), 0)
        ln = lax.broadcasted_iota(jnp.int32, (GRID_W, LANES), 1)
        kc = ln & (GRID_W - 1)
        upper = ln >= GRID_W
        rel = jnp.clip(kc - qc + (WIN_W - 1), 0, N_REL_COLS - 1)
        col_start = jnp.clip(qc - WIN_W // 2, 0, GRID_W - WIN_W)
        inside = (kc >= col_start) & (kc < col_start + WIN_W)
        base = head * (N_REL_ROWS * N_REL_COLS)

        def build(t, carry):
            acc = jnp.zeros((GRID_W, LANES), F32)
            for d in range(N_REL_COLS):
                lo = rb_ref[base + t * N_REL_COLS + d]
                hi = rb_ref[base + (t + 1) * N_REL_COLS + d]
                acc = jnp.where(rel == d, jnp.where(upper, hi, lo), acc)
            tab_ref[t] = jnp.where(inside, acc, NEG_INF)
            return carry

        lax.fori_loop(0, N_PAIR_TABLES, build, 0)

    def body(it, carry):
        r0 = it * NA_ROWS_PER_ITER
        row_ids = [r0 + u for u in range(NA_ROWS_PER_ITER)]
        starts = [jnp.clip(r - WIN_H // 2, 0, rows - WIN_H) for r in row_ids]
        scores = []
        for r, r_start in zip(row_ids, starts):
            q = q_ref[pl.ds(pl.multiple_of(r * GRID_W, GRID_W), GRID_W), :]
            kb = k_ref[pl.ds(pl.multiple_of(r_start * GRID_W, GRID_W), band), :]
            scores.append(lax.dot_general(q, kb, (((1,), (1,)), ((), ())), preferred_element_type=F32))
        probs, sums = [], []
        for r, r_start, s in zip(row_ids, starts, scores):
            rel0 = r_start - r + (WIN_H - 1)
            parts = [s[:, jj * LANES:(jj + 1) * LANES] * scale + tab_ref[rel0 + 2 * jj]
                     for jj in range(band // LANES)]
            logits = jnp.concatenate(parts, axis=-1)
            p = jnp.exp(logits - jnp.max(logits, axis=-1, keepdims=True))
            sums.append(jnp.sum(p, axis=-1, keepdims=True))
            probs.append(p.astype(BF16))
        outs = []
        for r_start, p, l in zip(starts, probs, sums):
            vb = v_ref[pl.ds(pl.multiple_of(r_start * GRID_W, GRID_W), band), :]
            outs.append((jnp.dot(p, vb, preferred_element_type=F32) / l).astype(o_ref.dtype))
        o_ref[pl.ds(pl.multiple_of(r0 * GRID_W, NA_ROWS_PER_ITER * GRID_W), NA_ROWS_PER_ITER * GRID_W), :] = (
            jnp.concatenate(outs, axis=0))
        return carry

    lax.fori_loop(0, rows // NA_ROWS_PER_ITER, body, 0)


def _na_attn(proj, rel_bias_flat, batch, seq):
    blk = lambda col0: pl.BlockSpec((seq, HEAD_DIM), lambda h, b: (b, col0 + h))
    return pl.pallas_call(
        _na_kernel,
        grid=(N_HEADS, batch),
        in_specs=[
            pl.BlockSpec(memory_space=pltpu.SMEM),
            blk(3 * N_HEADS), blk(4 * N_HEADS), blk(5 * N_HEADS),
        ],
        out_specs=pl.BlockSpec((seq, HEAD_DIM), lambda h, b: (b, h)),
        out_shape=jax.ShapeDtypeStruct((batch * seq, GROUP_WIDTH), BF16),
        scratch_shapes=[pltpu.VMEM((N_PAIR_TABLES, GRID_W, LANES), F32)],
        compiler_params=_compiler_params(("arbitrary", "arbitrary")),
        name="na_attn",
    )(rel_bias_flat, proj, proj, proj)


def _out_proj_kernel(x_ref, a_ref, n_ref, wa_ref, wn_ref, o_ref):
    acc = jnp.dot(a_ref[...], wa_ref[...], preferred_element_type=F32)
    acc = acc + jnp.dot(n_ref[...], wn_ref[...], preferred_element_type=F32)
    o_ref[...] = x_ref[...] + acc


def _out_proj(x2, a_out, n_out, w_out_bf16):
    rows = x2.shape[0]
    tm = OUT_TM
    return pl.pallas_call(
        _out_proj_kernel,
        grid=(rows // tm,),
        in_specs=[
            pl.BlockSpec((tm, D_MODEL), lambda i: (i, 0)),
            pl.BlockSpec((tm, GROUP_WIDTH), lambda i: (i, 0)),
            pl.BlockSpec((tm, GROUP_WIDTH), lambda i: (i, 0)),
            pl.BlockSpec((GROUP_WIDTH, D_MODEL), lambda i: (0, 0)),
            pl.BlockSpec((GROUP_WIDTH, D_MODEL), lambda i: (1, 0)),
        ],
        out_specs=pl.BlockSpec((tm, D_MODEL), lambda i: (i, 0)),
        out_shape=jax.ShapeDtypeStruct((rows, D_MODEL), F32),
        compiler_params=_compiler_params(("arbitrary",)),
        name="out_proj",
    )(x2, a_out, n_out, w_out_bf16, w_out_bf16)


def _mlp_kernel(x_ref, g_ref, wu_ref, wd_ref, gf_ref, o_ref, h_ref):
    j = pl.program_id(1)

    @pl.when(j == 0)
    def _():
        x = x_ref[...]
        h_ref[...] = (x * _rms_scale(x) * g_ref[...]).astype(BF16)
        o_ref[...] = x

    u = jnp.dot(h_ref[...], wu_ref[...], preferred_element_type=F32)
    r = jnp.maximum(u, 0.0)
    o_ref[...] += jnp.dot((r * r).astype(BF16), wd_ref[...], preferred_element_type=F32)

    @pl.when(j == pl.num_programs(1) - 1)
    def _():
        y = o_ref[...]
        o_ref[...] = y * _rms_scale(y) * gf_ref[...]


def _mlp(x1, g_mlp, w_up_bf16, w_down_bf16, g_final):
    rows = x1.shape[0]
    tm, tf = MLP_TM, MLP_TF
    return pl.pallas_call(
        _mlp_kernel,
        grid=(rows // tm, D_FF // tf),
        in_specs=[
            pl.BlockSpec((tm, D_MODEL), lambda i, j: (i, 0)),
            pl.BlockSpec((1, D_MODEL), lambda i, j: (0, 0)),
            pl.BlockSpec((D_MODEL, tf), lambda i, j: (0, j)),
            pl.BlockSpec((tf, D_MODEL), lambda i, j: (j, 0)),
            pl.BlockSpec((1, D_MODEL), lambda i, j: (0, 0)),
        ],
        out_specs=pl.BlockSpec((tm, D_MODEL), lambda i, j: (i, 0)),
        out_shape=jax.ShapeDtypeStruct((rows, D_MODEL), F32),
        scratch_shapes=[pltpu.VMEM((tm, D_MODEL), BF16)],
        compiler_params=_compiler_params(("arbitrary", "arbitrary")),
        name="mlp",
    )(x1, g_mlp, w_up_bf16, w_down_bf16, g_final)


def kernel(x, norm_mix_g, w_in, lambda_q1, lambda_k1, lambda_q2, lambda_k2, diff_subln_g, na_rel_bias, w_out,
           norm_mlp_g, w_up, w_down, norm_final_g):
    batch, seq, d_model = x.shape
    assert d_model == D_MODEL and w_in.shape == (1, D_MODEL, N_GROUPS * GROUP_WIDTH)
    assert seq % PROJ_TM == 0 and seq % ATTN_TQ == 0 and seq % ATTN_KC == 0 and seq % (GRID_W * WIN_H) == 0
    x2 = x.reshape(batch * seq, D_MODEL)

    proj = _in_proj(x2, norm_mix_g, w_in[0].astype(BF16), seq)
    a_out = _diff_attn(proj, lambda_q1, lambda_k1, lambda_q2, lambda_k2, diff_subln_g, batch, seq)
    n_out = _na_attn(proj, na_rel_bias[0].reshape(-1), batch, seq)
    x1 = _out_proj(x2, a_out, n_out, w_out[0].astype(BF16))
    y = _mlp(x1, norm_mlp_g, w_up[0].astype(BF16), w_down[0].astype(BF16), norm_final_g.reshape(1, D_MODEL))
    return y.reshape(batch, seq, D_MODEL)
```

```python
import functools
import math

import jax
import jax.numpy as jnp
from jax import lax
from jax.experimental import pallas as pl
from jax.experimental.pallas import tpu as pltpu

F32 = jnp.float32
BF16 = jnp.bfloat16

D_MODEL = 2048
HEAD_DIM = 128
N_HEADS = 8
GROUP_WIDTH = N_HEADS * HEAD_DIM
N_GROUPS = 6
DIFF_HALF = HEAD_DIM // 2
ROT_DIM = DIFF_HALF // 4
ROT_HALF = ROT_DIM // 2
ROPE_THETA = 500000.0
GRID_W = 64
WIN_H = 8
WIN_W = 16
D_FF = 4 * D_MODEL
EPS = 1e-5
NEG_INF = -1e30
LAMBDA_INIT = 0.8 - 0.6 * math.exp(-0.3 * 0)

LANES = 128
SUBLANES = 8
VMEM_LIMIT_BYTES = 56 * 1024 * 1024

PROJ_TM = 1024
ATTN_TQ = 256
ATTN_KC = 512
OUT_TM = 512
MLP_TM = 512
MLP_TF = 1024
NA_ROWS_PER_ITER = 16


def _compiler_params(semantics):
    return pltpu.CompilerParams(dimension_semantics=semantics, vmem_limit_bytes=VMEM_LIMIT_BYTES)


def _rms_scale(x):
    return lax.rsqrt(jnp.mean(x * x, axis=-1, keepdims=True) + EPS)


_HEAD_COL_SLICES = ((0, 8), (64, 72), (16, 40), (80, 104), (8, 16), (72, 80), (40, 64), (104, 128))
SCORE_SCALE_LOG2 = math.log2(math.e) / math.sqrt(DIFF_HALF)


def _is_comp1_lane(lane):
    half_lane = lane % DIFF_HALF
    return (half_lane < ROT_HALF) | ((half_lane >= ROT_DIM) & (half_lane < ROT_DIM + (DIFF_HALF - ROT_DIM) // 2))


def _permute_diff_heads(w_cols):
    d, n = w_cols.shape
    w3 = w_cols.reshape(d, n // HEAD_DIM, HEAD_DIM)
    return jnp.concatenate([w3[..., a:b] for a, b in _HEAD_COL_SLICES], axis=-1).reshape(d, n)


def _in_proj_kernel(x_ref, g_ref, w_ref, c_ref, s_ref, o_ref, h_ref):
    @pl.when(pl.program_id(1) == 0)
    def _():
        x = x_ref[...]
        h_ref[...] = (x * _rms_scale(x) * g_ref[...]).astype(BF16)

    acc = jnp.dot(h_ref[...], w_ref[...], preferred_element_type=F32)
    c = c_ref[...]
    s = s_ref[...]
    for hh in range(N_HEADS):
        a = acc[:, hh * HEAD_DIM:(hh + 1) * HEAD_DIM]
        o_ref[:, hh * HEAD_DIM:(hh + 1) * HEAD_DIM] = (a * c + pltpu.roll(a, LANES // 2, 1) * s).astype(BF16)


def _rotary_tables(seq, tm):
    lane = jnp.arange(LANES)
    inv_freq = jnp.power(ROPE_THETA, -jnp.arange(0, ROT_DIM, 2, dtype=F32) / ROT_DIM)
    ang = jnp.arange(seq, dtype=F32)[:, None] * inv_freq[lane % ROT_HALF][None, :]
    is_rot = (lane % DIFF_HALF) < ROT_DIM
    c = jnp.where(is_rot, jnp.cos(ang), 1.0)
    s = jnp.where(is_rot, jnp.where(lane < DIFF_HALF, -jnp.sin(ang), jnp.sin(ang)), 0.0)
    c_tab = jnp.concatenate([c * SCORE_SCALE_LOG2, c, jnp.ones((tm, LANES), F32)], axis=0)
    s_tab = jnp.concatenate([s * SCORE_SCALE_LOG2, s, jnp.zeros((tm, LANES), F32)], axis=0)
    return c_tab, s_tab


def _in_proj(x2, g, w_bf16, seq):
    rows = x2.shape[0]
    tm = PROJ_TM
    c_tab, s_tab = _rotary_tables(seq, tm)
    pos_blocks = seq // tm

    def tab_index(i, j):
        pos = i % pos_blocks
        return (jnp.where(j == 0, pos, jnp.where(j == 1, pos_blocks + pos, 2 * pos_blocks)), 0)

    tab_spec = pl.BlockSpec((tm, LANES), tab_index)
    return pl.pallas_call(
        _in_proj_kernel,
        grid=(rows // tm, N_GROUPS),
        in_specs=[
            pl.BlockSpec((tm, D_MODEL), lambda i, j: (i, 0)),
            pl.BlockSpec((1, D_MODEL), lambda i, j: (0, 0)),
            pl.BlockSpec((D_MODEL, GROUP_WIDTH), lambda i, j: (0, j)),
            tab_spec, tab_spec,
        ],
        out_specs=pl.BlockSpec((tm, GROUP_WIDTH), lambda i, j: (i, j)),
        out_shape=jax.ShapeDtypeStruct((rows, N_GROUPS * GROUP_WIDTH), BF16),
        scratch_shapes=[pltpu.VMEM((tm, D_MODEL), BF16)],
        compiler_params=_compiler_params(("arbitrary", "arbitrary")),
        name="in_proj",
    )(x2, g, w_bf16, c_tab, s_tab)


def _diff_attn_kernel(lq1_ref, lk1_ref, lq2_ref, lk2_ref, g_ref, q_ref, k_ref, v_ref, o_ref, vt_ref, s_ref):
    seq = k_ref.shape[0]
    tq = ATTN_TQ
    n_tiles = seq // tq
    n_chunks = seq // ATTN_KC
    fold = ATTN_KC // SUBLANES
    assert n_tiles % 2 == 0 and n_tiles >= 4

    vt_ref[...] = v_ref[...].T
    lam = (jnp.exp(jnp.sum(lq1_ref[...] * lk1_ref[...])) - jnp.exp(jnp.sum(lq2_ref[...] * lk2_ref[...]))
           + LAMBDA_INIT)
    comp1_lane = _is_comp1_lane(lax.broadcasted_iota(jnp.int32, (tq, HEAD_DIM), 1))

    def load_q_parts(t):
        q = q_ref[pl.ds(pl.multiple_of(t * tq, tq), tq), :]
        zero = jnp.zeros_like(q)
        return jnp.where(comp1_lane, q, zero), jnp.where(comp1_lane, zero, q)

    def score_chunk(slot, c, q_parts, m8):
        ks = k_ref[c * ATTN_KC:(c + 1) * ATTN_KC, :]
        new_m8 = []
        for comp in range(2):
            s = lax.dot_general(ks, q_parts[comp], (((1,), (1,)), ((), ())), preferred_element_type=F32)
            s_ref[slot, comp, c * ATTN_KC:(c + 1) * ATTN_KC, :] = s
            cm = jnp.max(s.reshape(fold, SUBLANES, tq), axis=0)
            new_m8.append(cm if m8 is None else jnp.maximum(m8[comp], cm))
        return new_m8

    def prob_chunk(slot, c, m, acc):
        vts = vt_ref[:, c * ATTN_KC:(c + 1) * ATTN_KC]
        new_acc = []
        for comp in range(2):
            l8, o = acc[comp]
            p = jnp.exp2(s_ref[slot, comp, c * ATTN_KC:(c + 1) * ATTN_KC, :] - m[comp])
            l8 = l8 + jnp.sum(p.reshape(fold, SUBLANES, tq), axis=0)
            o = o + jnp.dot(vts, p.astype(BF16), preferred_element_type=F32)
            new_acc.append((l8, o))
        return new_acc

    def col_max(m8):
        return [jnp.max(x, axis=0, keepdims=True) for x in m8]

    def zero_acc():
        return [(jnp.zeros((SUBLANES, tq), F32), jnp.zeros((HEAD_DIM, tq), F32)) for _ in range(2)]

    def finish(t, acc):
        outs = [o / jnp.sum(l8, axis=0, keepdims=True) for l8, o in acc]
        out = (outs[0] - lam * outs[1]).T
        y = out * _rms_scale(out) * g_ref[...]
        o_ref[pl.ds(pl.multiple_of(t * tq, tq), tq), :] = (y * (1.0 - LAMBDA_INIT)).astype(o_ref.dtype)

    def stage(t_next, t_cur, m_cur, t_prev, acc_prev):
        q_parts = load_q_parts(t_next[0]) if t_next is not None else None
        m8, acc = None, zero_acc()
        for c in range(n_chunks):
            if t_next is not None:
                m8 = score_chunk(t_next[1], c, q_parts, m8)
            if t_cur is not None:
                acc = prob_chunk(t_cur[1], c, m_cur, acc)
            if c == 0 and t_prev is not None:
                finish(t_prev, acc_prev)
        return (col_max(m8) if m8 is not None else None), acc

    m_cur, _ = stage((0, 0), None, None, None, None)
    m_cur, acc = stage((1, 1), (0, 0), m_cur, None, None)

    def pair(u, carry):
        m_odd, acc_even = carry
        m_even, acc_odd = stage((2 * u, 0), (2 * u - 1, 1), m_odd, 2 * u - 2, acc_even)
        m_odd, acc_even = stage((2 * u + 1, 1), (2 * u, 0), m_even, 2 * u - 1, acc_odd)
        return m_odd, acc_even

    m_cur, acc = lax.fori_loop(1, n_tiles // 2, pair, (m_cur, acc))
    _, acc_last = stage(None, (n_tiles - 1, 1), m_cur, n_tiles - 2, acc)
    finish(n_tiles - 1, acc_last)


def _diff_attn(proj, lq1, lk1, lq2, lk2, subln_g, batch, seq):
    lam_spec = pl.BlockSpec((1, DIFF_HALF), lambda b, h: (0, 0))
    head_blk = lambda col0: pl.BlockSpec((seq, HEAD_DIM), lambda b, h: (b, col0 + h))
    return pl.pallas_call(
        _diff_attn_kernel,
        grid=(batch, N_HEADS),
        in_specs=[
            lam_spec, lam_spec, lam_spec, lam_spec,
            pl.BlockSpec((1, HEAD_DIM), lambda b, h: (0, 0)),
            head_blk(0), head_blk(N_HEADS), head_blk(2 * N_HEADS),
        ],
        out_specs=pl.BlockSpec((seq, HEAD_DIM), lambda b, h: (b, h)),
        out_shape=jax.ShapeDtypeStruct((batch * seq, GROUP_WIDTH), BF16),
        scratch_shapes=[pltpu.VMEM((HEAD_DIM, seq), BF16), pltpu.VMEM((2, 2, seq, ATTN_TQ), F32)],
        compiler_params=_compiler_params(("arbitrary", "arbitrary")),
        name="diff_attn",
    )(lq1, lk1, lq2, lk2, subln_g, proj, proj, proj)


N_REL_ROWS = 2 * WIN_H - 1
N_REL_COLS = 2 * WIN_W - 1
N_PAIR_TABLES = N_REL_ROWS - 1


def _na_kernel(rb_ref, q_ref, k_ref, v_ref, o_ref, tab_ref):
    head = pl.program_id(0)
    rows = q_ref.shape[0] // GRID_W
    band = WIN_H * GRID_W
    scale = 1.0 / math.sqrt(HEAD_DIM)

    @pl.when(pl.program_id(1) == 0)
    def _():
        qc = lax.broadcasted_iota(jnp.int32, (GRID_W, LANES), 0)
        ln = lax.broadcasted_iota(jnp.int32, (GRID_W, LANES), 1)
        kc = ln & (GRID_W - 1)
        upper = ln >= GRID_W
        rel = jnp.clip(kc - qc + (WIN_W - 1), 0, N_REL_COLS - 1)
        col_start = jnp.clip(qc - WIN_W // 2, 0, GRID_W - WIN_W)
        inside = (kc >= col_start) & (kc < col_start + WIN_W)
        base = head * (N_REL_ROWS * N_REL_COLS)

        def build(t, carry):
            acc = jnp.zeros((GRID_W, LANES), F32)
            for d in range(N_REL_COLS):
                lo = rb_ref[base + t * N_REL_COLS + d]
                hi = rb_ref[base + (t + 1) * N_REL_COLS + d]
                acc = jnp.where(rel == d, jnp.where(upper, hi, lo), acc)
            tab_ref[t] = jnp.where(inside, acc, NEG_INF)
            return carry

        lax.fori_loop(0, N_PAIR_TABLES, build, 0)

    def body(it, carry):
        r0 = it * NA_ROWS_PER_ITER
        row_ids = [r0 + u for u in range(NA_ROWS_PER_ITER)]
        starts = [jnp.clip(r - WIN_H // 2, 0, rows - WIN_H) for r in row_ids]
        scores = []
        for r, r_start in zip(row_ids, starts):
            q = q_ref[pl.ds(pl.multiple_of(r * GRID_W, GRID_W), GRID_W), :]
            kb = k_ref[pl.ds(pl.multiple_of(r_start * GRID_W, GRID_W), band), :]
            scores.append(lax.dot_general(q, kb, (((1,), (1,)), ((), ())), preferred_element_type=F32))
        probs, sums = [], []
        for r, r_start, s in zip(row_ids, starts, scores):
            rel0 = r_start - r + (WIN_H - 1)
            parts = [s[:, jj * LANES:(jj + 1) * LANES] * scale + tab_ref[rel0 + 2 * jj]
                     for jj in range(band // LANES)]
            logits = jnp.concatenate(parts, axis=-1)
            p = jnp.exp(logits - jnp.max(logits, axis=-1, keepdims=True))
            sums.append(jnp.sum(p, axis=-1, keepdims=True))
            probs.append(p.astype(BF16))
        outs = []
        for r_start, p, l in zip(starts, probs, sums):
            vb = v_ref[pl.ds(pl.multiple_of(r_start * GRID_W, GRID_W), band), :]
            outs.append((jnp.dot(p, vb, preferred_element_type=F32) / l).astype(o_ref.dtype))
        o_ref[pl.ds(pl.multiple_of(r0 * GRID_W, NA_ROWS_PER_ITER * GRID_W), NA_ROWS_PER_ITER * GRID_W), :] = (
            jnp.concatenate(outs, axis=0))
        return carry

    lax.fori_loop(0, rows // NA_ROWS_PER_ITER, body, 0)


def _na_attn(proj, rel_bias_flat, batch, seq):
    blk = lambda col0: pl.BlockSpec((seq, HEAD_DIM), lambda h, b: (b, col0 + h))
    return pl.pallas_call(
        _na_kernel,
        grid=(N_HEADS, batch),
        in_specs=[
            pl.BlockSpec(memory_space=pltpu.SMEM),
            blk(3 * N_HEADS), blk(4 * N_HEADS), blk(5 * N_HEADS),
        ],
        out_specs=pl.BlockSpec((seq, HEAD_DIM), lambda h, b: (b, h)),
        out_shape=jax.ShapeDtypeStruct((batch * seq, GROUP_WIDTH), BF16),
        scratch_shapes=[pltpu.VMEM((N_PAIR_TABLES, GRID_W, LANES), F32)],
        compiler_params=_compiler_params(("arbitrary", "arbitrary")),
        name="na_attn",
    )(rel_bias_flat, proj, proj, proj)


def _out_proj_kernel(x_ref, a_ref, n_ref, wa_ref, wn_ref, o_ref):
    acc = jnp.dot(a_ref[...], wa_ref[...], preferred_element_type=F32)
    acc = acc + jnp.dot(n_ref[...], wn_ref[...], preferred_element_type=F32)
    o_ref[...] = x_ref[...] + acc


def _out_proj(x2, a_out, n_out, w_out_bf16):
    rows = x2.shape[0]
    tm = OUT_TM
    return pl.pallas_call(
        _out_proj_kernel,
        grid=(rows // tm,),
        in_specs=[
            pl.BlockSpec((tm, D_MODEL), lambda i: (i, 0)),
            pl.BlockSpec((tm, GROUP_WIDTH), lambda i: (i, 0)),
            pl.BlockSpec((tm, GROUP_WIDTH), lambda i: (i, 0)),
            pl.BlockSpec((GROUP_WIDTH, D_MODEL), lambda i: (0, 0)),
            pl.BlockSpec((GROUP_WIDTH, D_MODEL), lambda i: (1, 0)),
        ],
        out_specs=pl.BlockSpec((tm, D_MODEL), lambda i: (i, 0)),
        out_shape=jax.ShapeDtypeStruct((rows, D_MODEL), F32),
        compiler_params=_compiler_params(("arbitrary",)),
        name="out_proj",
    )(x2, a_out, n_out, w_out_bf16, w_out_bf16)


def _mlp_kernel(x_ref, g_ref, wu_ref, wd_ref, gf_ref, o_ref, h_ref):
    j = pl.program_id(1)

    @pl.when(j == 0)
    def _():
        x = x_ref[...]
        h_ref[...] = (x * _rms_scale(x) * g_ref[...]).astype(BF16)
        o_ref[...] = x

    u = jnp.dot(h_ref[...], wu_ref[...], preferred_element_type=F32)
    r = jnp.maximum(u, 0.0)
    o_ref[...] += jnp.dot((r * r).astype(BF16), wd_ref[...], preferred_element_type=F32)

    @pl.when(j == pl.num_programs(1) - 1)
    def _():
        y = o_ref[...]
        o_ref[...] = y * _rms_scale(y) * gf_ref[...]


def _mlp(x1, g_mlp, w_up_bf16, w_down_bf16, g_final):
    rows = x1.shape[0]
    tm, tf = MLP_TM, MLP_TF
    return pl.pallas_call(
        _mlp_kernel,
        grid=(rows // tm, D_FF // tf),
        in_specs=[
            pl.BlockSpec((tm, D_MODEL), lambda i, j: (i, 0)),
            pl.BlockSpec((1, D_MODEL), lambda i, j: (0, 0)),
            pl.BlockSpec((D_MODEL, tf), lambda i, j: (0, j)),
            pl.BlockSpec((tf, D_MODEL), lambda i, j: (j, 0)),
            pl.BlockSpec((1, D_MODEL), lambda i, j: (0, 0)),
        ],
        out_specs=pl.BlockSpec((tm, D_MODEL), lambda i, j: (i, 0)),
        out_shape=jax.ShapeDtypeStruct((rows, D_MODEL), F32),
        scratch_shapes=[pltpu.VMEM((tm, D_MODEL), BF16)],
        compiler_params=_compiler_params(("arbitrary", "arbitrary")),
        name="mlp",
    )(x1, g_mlp, w_up_bf16, w_down_bf16, g_final)


def kernel(x, norm_mix_g, w_in, lambda_q1, lambda_k1, lambda_q2, lambda_k2, diff_subln_g, na_rel_bias, w_out,
           norm_mlp_g, w_up, w_down, norm_final_g):
    batch, seq, d_model = x.shape
    assert d_model == D_MODEL and w_in.shape == (1, D_MODEL, N_GROUPS * GROUP_WIDTH)
    assert seq % PROJ_TM == 0 and seq % ATTN_TQ == 0 and seq % ATTN_KC == 0 and seq % (GRID_W * WIN_H) == 0
    x2 = x.reshape(batch * seq, D_MODEL)

    n_qk = 2 * GROUP_WIDTH
    w_in_bf16 = jnp.concatenate([_permute_diff_heads(w_in[0][:, :n_qk]), w_in[0][:, n_qk:]], axis=-1).astype(BF16)
    proj = _in_proj(x2, norm_mix_g, w_in_bf16, seq)
    a_out = _diff_attn(proj, lambda_q1, lambda_k1, lambda_q2, lambda_k2, diff_subln_g, batch, seq)
    n_out = _na_attn(proj, na_rel_bias[0].reshape(-1), batch, seq)
    x1 = _out_proj(x2, a_out, n_out, w_out[0].astype(BF16))
    y = _mlp(x1, norm_mlp_g, w_up[0].astype(BF16), w_down[0].astype(BF16), norm_final_g.reshape(1, D_MODEL))
    return y.reshape(batch, seq, D_MODEL)
```

```python
import functools
import math

import jax
import jax.numpy as jnp
import numpy as np
from jax import lax
from jax.experimental import pallas as pl
from jax.experimental.pallas import tpu as pltpu

F32 = jnp.float32
BF16 = jnp.bfloat16

D_MODEL = 2048
HEAD_DIM = 128
N_HEADS = 8
GROUP_WIDTH = N_HEADS * HEAD_DIM
N_GROUPS = 6
DIFF_HALF = HEAD_DIM // 2
ROT_DIM = DIFF_HALF // 4
ROT_HALF = ROT_DIM // 2
ROPE_THETA = 500000.0
GRID_W = 64
WIN_H = 8
WIN_W = 16
D_FF = 4 * D_MODEL
EPS = 1e-5
NEG_INF = -1e30
LAMBDA_INIT = 0.8 - 0.6 * math.exp(-0.3 * 0)

LANES = 128
SUBLANES = 8
VMEM_LIMIT_BYTES = 56 * 1024 * 1024

PROJ_TM = 1024
PROJ_MC = 256
ATTN_TQ = 256
ATTN_KC = 512
OUT_TM = 512
MLP_TM = 512
MLP_TF = 1024
NA_ROWS_PER_ITER = 16


def _compiler_params(semantics):
    return pltpu.CompilerParams(dimension_semantics=semantics, vmem_limit_bytes=VMEM_LIMIT_BYTES)


def _rms_scale(x):
    return lax.rsqrt(jnp.mean(x * x, axis=-1, keepdims=True) + EPS)


SCORE_SCALE_LOG2 = math.log2(math.e) / math.sqrt(DIFF_HALF)


def _in_proj_kernel(x_ref, g_ref, w_ref, c_ref, s_lo_ref, s_hi_ref, o_ref, h_ref):
    j = pl.program_id(1)

    @pl.when(j == 0)
    def _():
        x = x_ref[...]
        h_ref[...] = (x * _rms_scale(x) * g_ref[...]).astype(BF16)

    has_rotary = j < 2
    q_scale = jnp.where(j == 0, SCORE_SCALE_LOG2, 1.0).astype(F32)
    for mc in range(x_ref.shape[0] // PROJ_MC):
        rows = slice(mc * PROJ_MC, (mc + 1) * PROJ_MC)
        acc = jnp.dot(h_ref[rows, :], w_ref[...], preferred_element_type=F32)
        c = jnp.where(has_rotary, c_ref[rows, :] * q_scale, 1.0)
        s_lo = jnp.where(has_rotary, s_lo_ref[rows, :] * q_scale, 0.0)
        s_hi = jnp.where(has_rotary, s_hi_ref[rows, :] * q_scale, 0.0)
        for hh in range(N_HEADS):
            a = acc[:, hh * HEAD_DIM:(hh + 1) * HEAD_DIM]
            r = a * c + pltpu.roll(a, LANES - ROT_HALF, 1) * s_lo + pltpu.roll(a, ROT_HALF, 1) * s_hi
            o_ref[rows, hh * HEAD_DIM:(hh + 1) * HEAD_DIM] = r.astype(BF16)


@functools.lru_cache(maxsize=None)
def _rotary_lane_tables(seq):
    lane = np.arange(LANES)
    inv_freq = np.power(np.float32(ROPE_THETA), -np.arange(0, ROT_DIM, 2, dtype=np.float32) / np.float32(ROT_DIM))
    ang = (np.arange(seq, dtype=np.float32)[:, None] * inv_freq[lane % ROT_HALF][None, :]).astype(np.float32)
    in_lo = (lane % DIFF_HALF) < ROT_HALF
    in_hi = ((lane % DIFF_HALF) >= ROT_HALF) & ((lane % DIFF_HALF) < ROT_DIM)
    cos, sin = np.cos(ang).astype(np.float32), np.sin(ang).astype(np.float32)
    c = np.where(in_lo | in_hi, cos, np.float32(1.0))
    s_lo = np.where(in_lo, -sin, np.float32(0.0))
    s_hi = np.where(in_hi, sin, np.float32(0.0))
    return c, s_lo, s_hi


def _in_proj(x2, g, w_bf16, seq):
    rows = x2.shape[0]
    tm = PROJ_TM
    tables = [jnp.asarray(t) for t in _rotary_lane_tables(seq)]
    pos_blocks = seq // tm
    tab_spec = pl.BlockSpec((tm, LANES), lambda i, j: (i % pos_blocks, 0))
    return pl.pallas_call(
        _in_proj_kernel,
        grid=(rows // tm, N_GROUPS),
        in_specs=[
            pl.BlockSpec((tm, D_MODEL), lambda i, j: (i, 0)),
            pl.BlockSpec((1, D_MODEL), lambda i, j: (0, 0)),
            pl.BlockSpec((D_MODEL, GROUP_WIDTH), lambda i, j: (0, j)),
            tab_spec, tab_spec, tab_spec,
        ],
        out_specs=pl.BlockSpec((tm, GROUP_WIDTH), lambda i, j: (i, j)),
        out_shape=jax.ShapeDtypeStruct((rows, N_GROUPS * GROUP_WIDTH), BF16),
        scratch_shapes=[pltpu.VMEM((tm, D_MODEL), BF16)],
        compiler_params=_compiler_params(("arbitrary", "arbitrary")),
        name="in_proj",
    )(x2, g, w_bf16, *tables)


def _diff_attn_kernel(lq1_ref, lk1_ref, lq2_ref, lk2_ref, g_ref, q_ref, k_ref, v_ref, o_ref, vt_ref, s_ref):
    seq = k_ref.shape[0]
    tq = ATTN_TQ
    n_tiles = seq // tq
    n_chunks = seq // ATTN_KC
    fold = ATTN_KC // SUBLANES
    assert n_tiles % 2 == 0 and n_tiles >= 4

    vt_ref[...] = v_ref[...].T
    lam = (jnp.exp(jnp.sum(lq1_ref[...] * lk1_ref[...])) - jnp.exp(jnp.sum(lq2_ref[...] * lk2_ref[...]))
           + LAMBDA_INIT)
    comp1_lane = lax.broadcasted_iota(jnp.int32, (tq, HEAD_DIM), 1) < DIFF_HALF

    def load_q_parts(t):
        q = q_ref[pl.ds(pl.multiple_of(t * tq, tq), tq), :]
        zero = jnp.zeros_like(q)
        return jnp.where(comp1_lane, q, zero), jnp.where(comp1_lane, zero, q)

    def score_chunk(slot, c, q_parts, m8):
        ks = k_ref[c * ATTN_KC:(c + 1) * ATTN_KC, :]
        new_m8 = []
        for comp in range(2):
            s = lax.dot_general(ks, q_parts[comp], (((1,), (1,)), ((), ())), preferred_element_type=F32)
            s_ref[slot, comp, c * ATTN_KC:(c + 1) * ATTN_KC, :] = s
            cm = jnp.max(s.reshape(fold, SUBLANES, tq), axis=0)
            new_m8.append(cm if m8 is None else jnp.maximum(m8[comp], cm))
        return new_m8

    def prob_chunk(slot, c, m, acc):
        vts = vt_ref[:, c * ATTN_KC:(c + 1) * ATTN_KC]
        new_acc = []
        for comp in range(2):
            l8, o = acc[comp]
            p = jnp.exp2(s_ref[slot, comp, c * ATTN_KC:(c + 1) * ATTN_KC, :] - m[comp])
            l8 = l8 + jnp.sum(p.reshape(fold, SUBLANES, tq), axis=0)
            o = o + jnp.dot(vts, p.astype(BF16), preferred_element_type=F32)
            new_acc.append((l8, o))
        return new_acc

    def col_max(m8):
        return [jnp.max(x, axis=0, keepdims=True) for x in m8]

    def zero_acc():
        return [(jnp.zeros((SUBLANES, tq), F32), jnp.zeros((HEAD_DIM, tq), F32)) for _ in range(2)]

    def finish(t, acc):
        outs = [o / jnp.sum(l8, axis=0, keepdims=True) for l8, o in acc]
        out = (outs[0] - lam * outs[1]).T
        y = out * _rms_scale(out) * g_ref[...]
        o_ref[pl.ds(pl.multiple_of(t * tq, tq), tq), :] = (y * (1.0 - LAMBDA_INIT)).astype(o_ref.dtype)

    def stage(t_next, t_cur, m_cur, t_prev, acc_prev):
        q_parts = load_q_parts(t_next[0]) if t_next is not None else None
        m8, acc = None, zero_acc()
        for c in range(n_chunks):
            if t_next is not None:
                m8 = score_chunk(t_next[1], c, q_parts, m8)
            if t_cur is not None:
                acc = prob_chunk(t_cur[1], c, m_cur, acc)
            if c == 0 and t_prev is not None:
                finish(t_prev, acc_prev)
        return (col_max(m8) if m8 is not None else None), acc

    m_cur, _ = stage((0, 0), None, None, None, None)
    m_cur, acc = stage((1, 1), (0, 0), m_cur, None, None)

    def pair(u, carry):
        m_odd, acc_even = carry
        m_even, acc_odd = stage((2 * u, 0), (2 * u - 1, 1), m_odd, 2 * u - 2, acc_even)
        m_odd, acc_even = stage((2 * u + 1, 1), (2 * u, 0), m_even, 2 * u - 1, acc_odd)
        return m_odd, acc_even

    m_cur, acc = lax.fori_loop(1, n_tiles // 2, pair, (m_cur, acc))
    _, acc_last = stage(None, (n_tiles - 1, 1), m_cur, n_tiles - 2, acc)
    finish(n_tiles - 1, acc_last)


def _diff_attn(proj, lq1, lk1, lq2, lk2, subln_g, batch, seq):
    lam_spec = pl.BlockSpec((1, DIFF_HALF), lambda b, h: (0, 0))
    head_blk = lambda col0: pl.BlockSpec((seq, HEAD_DIM), lambda b, h: (b, col0 + h))
    return pl.pallas_call(
        _diff_attn_kernel,
        grid=(batch, N_HEADS),
        in_specs=[
            lam_spec, lam_spec, lam_spec, lam_spec,
            pl.BlockSpec((1, HEAD_DIM), lambda b, h: (0, 0)),
            head_blk(0), head_blk(N_HEADS), head_blk(2 * N_HEADS),
        ],
        out_specs=pl.BlockSpec((seq, HEAD_DIM), lambda b, h: (b, h)),
        out_shape=jax.ShapeDtypeStruct((batch * seq, GROUP_WIDTH), BF16),
        scratch_shapes=[pltpu.VMEM((HEAD_DIM, seq), BF16), pltpu.VMEM((2, 2, seq, ATTN_TQ), F32)],
        compiler_params=_compiler_params(("arbitrary", "arbitrary")),
        name="diff_attn",
    )(lq1, lk1, lq2, lk2, subln_g, proj, proj, proj)


N_REL_ROWS = 2 * WIN_H - 1
N_REL_COLS = 2 * WIN_W - 1
N_PAIR_TABLES = N_REL_ROWS - 1


def _na_kernel(rb_ref, q_ref, k_ref, v_ref, o_ref, tab_ref):
    head = pl.program_id(0)
    rows = q_ref.shape[0] // GRID_W
    band = WIN_H * GRID_W
    scale = 1.0 / math.sqrt(HEAD_DIM)

    @pl.when(pl.program_id(1) == 0)
    def _():
        qc = lax.broadcasted_iota(jnp.int32, (GRID_W, LANES), 0)
        ln = lax.broadcasted_iota(jnp.int32, (GRID_W, LANES), 1)
        kc = ln & (GRID_W - 1)
        upper = ln >= GRID_W
        rel = jnp.clip(kc - qc + (WIN_W - 1), 0, N_REL_COLS - 1)
        col_start = jnp.clip(qc - WIN_W // 2, 0, GRID_W - WIN_W)
        inside = (kc >= col_start) & (kc < col_start + WIN_W)
        base = head * (N_REL_ROWS * N_REL_COLS)

        def build(t, carry):
            acc = jnp.zeros((GRID_W, LANES), F32)
            for d in range(N_REL_COLS):
                lo = rb_ref[base + t * N_REL_COLS + d]
                hi = rb_ref[base + (t + 1) * N_REL_COLS + d]
                acc = jnp.where(rel == d, jnp.where(upper, hi, lo), acc)
            tab_ref[t] = jnp.where(inside, acc, NEG_INF)
            return carry

        lax.fori_loop(0, N_PAIR_TABLES, build, 0)

    def body(it, carry):
        r0 = it * NA_ROWS_PER_ITER
        row_ids = [r0 + u for u in range(NA_ROWS_PER_ITER)]
        starts = [jnp.clip(r - WIN_H // 2, 0, rows - WIN_H) for r in row_ids]
        scores = []
        for r, r_start in zip(row_ids, starts):
            q = q_ref[pl.ds(pl.multiple_of(r * GRID_W, GRID_W), GRID_W), :]
            kb = k_ref[pl.ds(pl.multiple_of(r_start * GRID_W, GRID_W), band), :]
            scores.append(lax.dot_general(q, kb, (((1,), (1,)), ((), ())), preferred_element_type=F32))
        probs, sums = [], []
        for r, r_start, s in zip(row_ids, starts, scores):
            rel0 = r_start - r + (WIN_H - 1)
            parts = [s[:, jj * LANES:(jj + 1) * LANES] * scale + tab_ref[rel0 + 2 * jj]
                     for jj in range(band // LANES)]
            logits = jnp.concatenate(parts, axis=-1)
            p = jnp.exp(logits - jnp.max(logits, axis=-1, keepdims=True))
            sums.append(jnp.sum(p, axis=-1, keepdims=True))
            probs.append(p.astype(BF16))
        outs = []
        for r_start, p, l in zip(starts, probs, sums):
            vb = v_ref[pl.ds(pl.multiple_of(r_start * GRID_W, GRID_W), band), :]
            outs.append((jnp.dot(p, vb, preferred_element_type=F32) / l).astype(o_ref.dtype))
        o_ref[pl.ds(pl.multiple_of(r0 * GRID_W, NA_ROWS_PER_ITER * GRID_W), NA_ROWS_PER_ITER * GRID_W), :] = (
            jnp.concatenate(outs, axis=0))
        return carry

    lax.fori_loop(0, rows // NA_ROWS_PER_ITER, body, 0)


def _na_attn(proj, rel_bias_flat, batch, seq):
    blk = lambda col0: pl.BlockSpec((seq, HEAD_DIM), lambda h, b: (b, col0 + h))
    return pl.pallas_call(
        _na_kernel,
        grid=(N_HEADS, batch),
        in_specs=[
            pl.BlockSpec(memory_space=pltpu.SMEM),
            blk(3 * N_HEADS), blk(4 * N_HEADS), blk(5 * N_HEADS),
        ],
        out_specs=pl.BlockSpec((seq, HEAD_DIM), lambda h, b: (b, h)),
        out_shape=jax.ShapeDtypeStruct((batch * seq, GROUP_WIDTH), BF16),
        scratch_shapes=[pltpu.VMEM((N_PAIR_TABLES, GRID_W, LANES), F32)],
        compiler_params=_compiler_params(("arbitrary", "arbitrary")),
        name="na_attn",
    )(rel_bias_flat, proj, proj, proj)


def _out_proj_kernel(x_ref, a_ref, n_ref, wa_ref, wn_ref, o_ref):
    acc = jnp.dot(a_ref[...], wa_ref[...], preferred_element_type=F32)
    acc = acc + jnp.dot(n_ref[...], wn_ref[...], preferred_element_type=F32)
    o_ref[...] = x_ref[...] + acc


def _out_proj(x2, a_out, n_out, w_out_bf16):
    rows = x2.shape[0]
    tm = OUT_TM
    return pl.pallas_call(
        _out_proj_kernel,
        grid=(rows // tm,),
        in_specs=[
            pl.BlockSpec((tm, D_MODEL), lambda i: (i, 0)),
            pl.BlockSpec((tm, GROUP_WIDTH), lambda i: (i, 0)),
            pl.BlockSpec((tm, GROUP_WIDTH), lambda i: (i, 0)),
            pl.BlockSpec((GROUP_WIDTH, D_MODEL), lambda i: (0, 0)),
            pl.BlockSpec((GROUP_WIDTH, D_MODEL), lambda i: (1, 0)),
        ],
        out_specs=pl.BlockSpec((tm, D_MODEL), lambda i: (i, 0)),
        out_shape=jax.ShapeDtypeStruct((rows, D_MODEL), F32),
        compiler_params=_compiler_params(("arbitrary",)),
        name="out_proj",
    )(x2, a_out, n_out, w_out_bf16, w_out_bf16)


def _mlp_kernel(x_ref, g_ref, wu_ref, wd_ref, gf_ref, o_ref, h_ref):
    j = pl.program_id(1)

    @pl.when(j == 0)
    def _():
        x = x_ref[...]
        h_ref[...] = (x * _rms_scale(x) * g_ref[...]).astype(BF16)
        o_ref[...] = x

    u = jnp.dot(h_ref[...], wu_ref[...], preferred_element_type=F32)
    r = jnp.maximum(u, 0.0)
    o_ref[...] += jnp.dot((r * r).astype(BF16), wd_ref[...], preferred_element_type=F32)

    @pl.when(j == pl.num_programs(1) - 1)
    def _():
        y = o_ref[...]
        o_ref[...] = y * _rms_scale(y) * gf_ref[...]


def _mlp(x1, g_mlp, w_up_bf16, w_down_bf16, g_final):
    rows = x1.shape[0]
    tm, tf = MLP_TM, MLP_TF
    return pl.pallas_call(
        _mlp_kernel,
        grid=(rows // tm, D_FF // tf),
        in_specs=[
            pl.BlockSpec((tm, D_MODEL), lambda i, j: (i, 0)),
            pl.BlockSpec((1, D_MODEL), lambda i, j: (0, 0)),
            pl.BlockSpec((D_MODEL, tf), lambda i, j: (0, j)),
            pl.BlockSpec((tf, D_MODEL), lambda i, j: (j, 0)),
            pl.BlockSpec((1, D_MODEL), lambda i, j: (0, 0)),
        ],
        out_specs=pl.BlockSpec((tm, D_MODEL), lambda i, j: (i, 0)),
        out_shape=jax.ShapeDtypeStruct((rows, D_MODEL), F32),
        scratch_shapes=[pltpu.VMEM((tm, D_MODEL), BF16)],
        compiler_params=_compiler_params(("arbitrary", "arbitrary")),
        name="mlp",
    )(x1, g_mlp, w_up_bf16, w_down_bf16, g_final)


def kernel(x, norm_mix_g, w_in, lambda_q1, lambda_k1, lambda_q2, lambda_k2, diff_subln_g, na_rel_bias, w_out,
           norm_mlp_g, w_up, w_down, norm_final_g):
    batch, seq, d_model = x.shape
    assert d_model == D_MODEL and w_in.shape == (1, D_MODEL, N_GROUPS * GROUP_WIDTH)
    assert seq % PROJ_TM == 0 and seq % ATTN_TQ == 0 and seq % ATTN_KC == 0 and seq % (GRID_W * WIN_H) == 0
    x2 = x.reshape(batch * seq, D_MODEL)

    proj = _in_proj(x2, norm_mix_g, w_in[0].astype(BF16), seq)
    a_out = _diff_attn(proj, lambda_q1, lambda_k1, lambda_q2, lambda_k2, diff_subln_g, batch, seq)
    n_out = _na_attn(proj, na_rel_bias[0].reshape(-1), batch, seq)
    x1 = _out_proj(x2, a_out, n_out, w_out[0].astype(BF16))
    y = _mlp(x1, norm_mlp_g, w_up[0].astype(BF16), w_down[0].astype(BF16), norm_final_g.reshape(1, D_MODEL))
    return y.reshape(batch, seq, D_MODEL)
```

```python
import functools
import math

import jax
import jax.numpy as jnp
import numpy as np
from jax import lax
from jax.experimental import pallas as pl
from jax.experimental.pallas import tpu as pltpu

F32 = jnp.float32
BF16 = jnp.bfloat16

D_MODEL = 2048
HEAD_DIM = 128
N_HEADS = 8
GROUP_WIDTH = N_HEADS * HEAD_DIM
N_GROUPS = 6
DIFF_HALF = HEAD_DIM // 2
ROT_DIM = DIFF_HALF // 4
ROT_HALF = ROT_DIM // 2
ROPE_THETA = 500000.0
GRID_W = 64
WIN_H = 8
WIN_W = 16
D_FF = 4 * D_MODEL
EPS = 1e-5
NEG_INF = -1e30
LAMBDA_INIT = 0.8 - 0.6 * math.exp(-0.3 * 0)

LANES = 128
SUBLANES = 8
VMEM_LIMIT_BYTES = 56 * 1024 * 1024

PROJ_TM = 1024
PROJ_MC = 256
ATTN_TQ = 256
ATTN_KC = 512
OUT_TM = 512
OUT_MC = 128
MLP_TM = 1024
MLP_TF = 512
MLP_MC = 256
NA_ROWS_PER_ITER = 16


def _compiler_params(semantics):
    return pltpu.CompilerParams(dimension_semantics=semantics, vmem_limit_bytes=VMEM_LIMIT_BYTES)


def _rms_scale(x):
    return lax.rsqrt(jnp.mean(x * x, axis=-1, keepdims=True) + EPS)


SCORE_SCALE_LOG2 = math.log2(math.e) / math.sqrt(DIFF_HALF)


def _in_proj_kernel(x_ref, g_ref, w_ref, c_ref, s_lo_ref, s_hi_ref, o_ref, h_ref):
    j = pl.program_id(1)

    def step(first):
        has_rotary = True if first else j < 2
        q_scale = SCORE_SCALE_LOG2 if first else 1.0
        w = w_ref[...].astype(BF16)
        for mc in range(x_ref.shape[0] // PROJ_MC):
            rows = slice(mc * PROJ_MC, (mc + 1) * PROJ_MC)
            if first:
                x = x_ref[rows, :]
                h_ref[rows, :] = (x * _rms_scale(x) * g_ref[...]).astype(BF16)
            acc = jnp.dot(h_ref[rows, :], w, preferred_element_type=F32)
            c = jnp.where(has_rotary, c_ref[rows, :] * q_scale, 1.0)
            s_lo = jnp.where(has_rotary, s_lo_ref[rows, :] * q_scale, 0.0)
            s_hi = jnp.where(has_rotary, s_hi_ref[rows, :] * q_scale, 0.0)
            for hh in range(N_HEADS):
                a = acc[:, hh * HEAD_DIM:(hh + 1) * HEAD_DIM]
                r = a * c + pltpu.roll(a, LANES - ROT_HALF, 1) * s_lo + pltpu.roll(a, ROT_HALF, 1) * s_hi
                o_ref[rows, hh * HEAD_DIM:(hh + 1) * HEAD_DIM] = r.astype(BF16)

    pl.when(j == 0)(functools.partial(step, True))
    pl.when(j > 0)(functools.partial(step, False))


@functools.lru_cache(maxsize=None)
def _rotary_lane_tables(seq):
    lane = np.arange(LANES)
    inv_freq = np.power(np.float32(ROPE_THETA), -np.arange(0, ROT_DIM, 2, dtype=np.float32) / np.float32(ROT_DIM))
    ang = (np.arange(seq, dtype=np.float32)[:, None] * inv_freq[lane % ROT_HALF][None, :]).astype(np.float32)
    in_lo = (lane % DIFF_HALF) < ROT_HALF
    in_hi = ((lane % DIFF_HALF) >= ROT_HALF) & ((lane % DIFF_HALF) < ROT_DIM)
    cos, sin = np.cos(ang).astype(np.float32), np.sin(ang).astype(np.float32)
    c = np.where(in_lo | in_hi, cos, np.float32(1.0))
    s_lo = np.where(in_lo, -sin, np.float32(0.0))
    s_hi = np.where(in_hi, sin, np.float32(0.0))
    return c, s_lo, s_hi


def _in_proj(x2, g, w_f32, seq):
    rows = x2.shape[0]
    tm = PROJ_TM
    tables = [jnp.asarray(t) for t in _rotary_lane_tables(seq)]
    pos_blocks = seq // tm
    tab_spec = pl.BlockSpec((tm, LANES), lambda i, j: (i % pos_blocks, 0))
    return pl.pallas_call(
        _in_proj_kernel,
        grid=(rows // tm, N_GROUPS),
        in_specs=[
            pl.BlockSpec((tm, D_MODEL), lambda i, j: (i, 0)),
            pl.BlockSpec((1, D_MODEL), lambda i, j: (0, 0)),
            pl.BlockSpec((D_MODEL, GROUP_WIDTH), lambda i, j: (0, j)),
            tab_spec, tab_spec, tab_spec,
        ],
        out_specs=pl.BlockSpec((tm, GROUP_WIDTH), lambda i, j: (i, j)),
        out_shape=jax.ShapeDtypeStruct((rows, N_GROUPS * GROUP_WIDTH), BF16),
        scratch_shapes=[pltpu.VMEM((tm, D_MODEL), BF16)],
        compiler_params=_compiler_params(("arbitrary", "arbitrary")),
        name="in_proj",
    )(x2, g, w_f32, *tables)


def _diff_attn_kernel(lq1_ref, lk1_ref, lq2_ref, lk2_ref, g_ref, q_ref, k_ref, v_ref, o_ref, vt_ref, s_ref):
    seq = k_ref.shape[0]
    tq = ATTN_TQ
    n_tiles = seq // tq
    n_chunks = seq // ATTN_KC
    fold = ATTN_KC // SUBLANES
    assert n_tiles % 2 == 0 and n_tiles >= 4

    vt_ref[...] = v_ref[...].T
    lam = (jnp.exp(jnp.sum(lq1_ref[...] * lk1_ref[...])) - jnp.exp(jnp.sum(lq2_ref[...] * lk2_ref[...]))
           + LAMBDA_INIT)
    comp1_lane = lax.broadcasted_iota(jnp.int32, (tq, HEAD_DIM), 1) < DIFF_HALF

    def load_q_parts(t):
        q = q_ref[pl.ds(pl.multiple_of(t * tq, tq), tq), :]
        zero = jnp.zeros_like(q)
        return jnp.where(comp1_lane, q, zero), jnp.where(comp1_lane, zero, q)

    def score_chunk(slot, c, q_parts, m8):
        ks = k_ref[c * ATTN_KC:(c + 1) * ATTN_KC, :]
        new_m8 = []
        for comp in range(2):
            s = lax.dot_general(ks, q_parts[comp], (((1,), (1,)), ((), ())), preferred_element_type=F32)
            s_ref[slot, comp, c * ATTN_KC:(c + 1) * ATTN_KC, :] = s
            cm = jnp.max(s.reshape(fold, SUBLANES, tq), axis=0)
            new_m8.append(cm if m8 is None else jnp.maximum(m8[comp], cm))
        return new_m8

    def prob_chunk(slot, c, m, acc):
        vts = vt_ref[:, c * ATTN_KC:(c + 1) * ATTN_KC]
        new_acc = []
        for comp in range(2):
            l8, o = acc[comp]
            p = jnp.exp2(s_ref[slot, comp, c * ATTN_KC:(c + 1) * ATTN_KC, :] - m[comp])
            l8 = l8 + jnp.sum(p.reshape(fold, SUBLANES, tq), axis=0)
            o = o + jnp.dot(vts, p.astype(BF16), preferred_element_type=F32)
            new_acc.append((l8, o))
        return new_acc

    def col_max(m8):
        return [jnp.max(x, axis=0, keepdims=True) for x in m8]

    def zero_acc():
        return [(jnp.zeros((SUBLANES, tq), F32), jnp.zeros((HEAD_DIM, tq), F32)) for _ in range(2)]

    def finish(t, acc):
        outs = [o / jnp.sum(l8, axis=0, keepdims=True) for l8, o in acc]
        out = (outs[0] - lam * outs[1]).T
        y = out * _rms_scale(out) * g_ref[...]
        o_ref[pl.ds(pl.multiple_of(t * tq, tq), tq), :] = (y * (1.0 - LAMBDA_INIT)).astype(o_ref.dtype)

    def stage(t_next, t_cur, m_cur, t_prev, acc_prev):
        q_parts = load_q_parts(t_next[0]) if t_next is not None else None
        m8, acc = None, zero_acc()
        for c in range(n_chunks):
            if t_next is not None:
                m8 = score_chunk(t_next[1], c, q_parts, m8)
            if t_cur is not None:
                acc = prob_chunk(t_cur[1], c, m_cur, acc)
            if c == 0 and t_prev is not None:
                finish(t_prev, acc_prev)
        return (col_max(m8) if m8 is not None else None), acc

    m_cur, _ = stage((0, 0), None, None, None, None)
    m_cur, acc = stage((1, 1), (0, 0), m_cur, None, None)

    def pair(u, carry):
        m_odd, acc_even = carry
        m_even, acc_odd = stage((2 * u, 0), (2 * u - 1, 1), m_odd, 2 * u - 2, acc_even)
        m_odd, acc_even = stage((2 * u + 1, 1), (2 * u, 0), m_even, 2 * u - 1, acc_odd)
        return m_odd, acc_even

    m_cur, acc = lax.fori_loop(1, n_tiles // 2, pair, (m_cur, acc))
    _, acc_last = stage(None, (n_tiles - 1, 1), m_cur, n_tiles - 2, acc)
    finish(n_tiles - 1, acc_last)


def _diff_attn(proj, lq1, lk1, lq2, lk2, subln_g, batch, seq):
    lam_spec = pl.BlockSpec((1, DIFF_HALF), lambda b, h: (0, 0))
    head_blk = lambda col0: pl.BlockSpec((seq, HEAD_DIM), lambda b, h: (b, col0 + h))
    return pl.pallas_call(
        _diff_attn_kernel,
        grid=(batch, N_HEADS),
        in_specs=[
            lam_spec, lam_spec, lam_spec, lam_spec,
            pl.BlockSpec((1, HEAD_DIM), lambda b, h: (0, 0)),
            head_blk(0), head_blk(N_HEADS), head_blk(2 * N_HEADS),
        ],
        out_specs=pl.BlockSpec((seq, HEAD_DIM), lambda b, h: (b, h)),
        out_shape=jax.ShapeDtypeStruct((batch * seq, GROUP_WIDTH), BF16),
        scratch_shapes=[pltpu.VMEM((HEAD_DIM, seq), BF16), pltpu.VMEM((2, 2, seq, ATTN_TQ), F32)],
        compiler_params=_compiler_params(("arbitrary", "arbitrary")),
        name="diff_attn",
    )(lq1, lk1, lq2, lk2, subln_g, proj, proj, proj)


N_REL_ROWS = 2 * WIN_H - 1
N_REL_COLS = 2 * WIN_W - 1
N_PAIR_TABLES = N_REL_ROWS - 1


def _na_kernel(rb_ref, q_ref, k_ref, v_ref, o_ref, tab_ref):
    head = pl.program_id(0)
    rows = q_ref.shape[0] // GRID_W
    band = WIN_H * GRID_W
    scale = 1.0 / math.sqrt(HEAD_DIM)

    @pl.when(pl.program_id(1) == 0)
    def _():
        qc = lax.broadcasted_iota(jnp.int32, (GRID_W, LANES), 0)
        ln = lax.broadcasted_iota(jnp.int32, (GRID_W, LANES), 1)
        kc = ln & (GRID_W - 1)
        upper = ln >= GRID_W
        rel = jnp.clip(kc - qc + (WIN_W - 1), 0, N_REL_COLS - 1)
        col_start = jnp.clip(qc - WIN_W // 2, 0, GRID_W - WIN_W)
        inside = (kc >= col_start) & (kc < col_start + WIN_W)
        base = head * (N_REL_ROWS * N_REL_COLS)

        def build(t, carry):
            acc = jnp.zeros((GRID_W, LANES), F32)
            for d in range(N_REL_COLS):
                lo = rb_ref[base + t * N_REL_COLS + d]
                hi = rb_ref[base + (t + 1) * N_REL_COLS + d]
                acc = jnp.where(rel == d, jnp.where(upper, hi, lo), acc)
            tab_ref[t] = jnp.where(inside, acc, NEG_INF)
            return carry

        lax.fori_loop(0, N_PAIR_TABLES, build, 0)

    def body(it, carry):
        r0 = it * NA_ROWS_PER_ITER
        row_ids = [r0 + u for u in range(NA_ROWS_PER_ITER)]
        starts = [jnp.clip(r - WIN_H // 2, 0, rows - WIN_H) for r in row_ids]
        scores = []
        for r, r_start in zip(row_ids, starts):
            q = q_ref[pl.ds(pl.multiple_of(r * GRID_W, GRID_W), GRID_W), :]
            kb = k_ref[pl.ds(pl.multiple_of(r_start * GRID_W, GRID_W), band), :]
            scores.append(lax.dot_general(q, kb, (((1,), (1,)), ((), ())), preferred_element_type=F32))
        probs, sums = [], []
        for r, r_start, s in zip(row_ids, starts, scores):
            rel0 = r_start - r + (WIN_H - 1)
            parts = [s[:, jj * LANES:(jj + 1) * LANES] * scale + tab_ref[rel0 + 2 * jj]
                     for jj in range(band // LANES)]
            logits = jnp.concatenate(parts, axis=-1)
            p = jnp.exp(logits - jnp.max(logits, axis=-1, keepdims=True))
            sums.append(jnp.sum(p, axis=-1, keepdims=True))
            probs.append(p.astype(BF16))
        outs = []
        for r_start, p, l in zip(starts, probs, sums):
            vb = v_ref[pl.ds(pl.multiple_of(r_start * GRID_W, GRID_W), band), :]
            outs.append((jnp.dot(p, vb, preferred_element_type=F32) / l).astype(o_ref.dtype))
        o_ref[pl.ds(pl.multiple_of(r0 * GRID_W, NA_ROWS_PER_ITER * GRID_W), NA_ROWS_PER_ITER * GRID_W), :] = (
            jnp.concatenate(outs, axis=0))
        return carry

    lax.fori_loop(0, rows // NA_ROWS_PER_ITER, body, 0)


def _na_attn(proj, rel_bias_flat, batch, seq):
    blk = lambda col0: pl.BlockSpec((seq, HEAD_DIM), lambda h, b: (b, col0 + h))
    return pl.pallas_call(
        _na_kernel,
        grid=(N_HEADS, batch),
        in_specs=[
            pl.BlockSpec(memory_space=pltpu.SMEM),
            blk(3 * N_HEADS), blk(4 * N_HEADS), blk(5 * N_HEADS),
        ],
        out_specs=pl.BlockSpec((seq, HEAD_DIM), lambda h, b: (b, h)),
        out_shape=jax.ShapeDtypeStruct((batch * seq, GROUP_WIDTH), BF16),
        scratch_shapes=[pltpu.VMEM((N_PAIR_TABLES, GRID_W, LANES), F32)],
        compiler_params=_compiler_params(("arbitrary", "arbitrary")),
        name="na_attn",
    )(rel_bias_flat, proj, proj, proj)


def _out_proj_kernel(x_ref, a_ref, n_ref, wa_ref, wn_ref, g_ref, x1_ref, h_ref):
    for mc in range(x_ref.shape[0] // OUT_MC):
        rows = slice(mc * OUT_MC, (mc + 1) * OUT_MC)
        acc = jnp.dot(a_ref[rows, :], wa_ref[...], preferred_element_type=F32)
        acc = acc + jnp.dot(n_ref[rows, :], wn_ref[...], preferred_element_type=F32)
        x1 = x_ref[rows, :] + acc
        x1_ref[rows, :] = x1
        h_ref[rows, :] = (x1 * _rms_scale(x1) * g_ref[...]).astype(BF16)


def _out_proj(x2, a_out, n_out, w_out_bf16, g_mlp):
    rows = x2.shape[0]
    tm = OUT_TM
    row_blk = lambda width: pl.BlockSpec((tm, width), lambda i: (i, 0))
    return pl.pallas_call(
        _out_proj_kernel,
        grid=(rows // tm,),
        in_specs=[
            row_blk(D_MODEL), row_blk(GROUP_WIDTH), row_blk(GROUP_WIDTH),
            pl.BlockSpec((GROUP_WIDTH, D_MODEL), lambda i: (0, 0)),
            pl.BlockSpec((GROUP_WIDTH, D_MODEL), lambda i: (1, 0)),
            pl.BlockSpec((1, D_MODEL), lambda i: (0, 0)),
        ],
        out_specs=[row_blk(D_MODEL), row_blk(D_MODEL)],
        out_shape=[jax.ShapeDtypeStruct((rows, D_MODEL), F32), jax.ShapeDtypeStruct((rows, D_MODEL), BF16)],
        compiler_params=_compiler_params(("arbitrary",)),
        name="out_proj",
    )(x2, a_out, n_out, w_out_bf16, w_out_bf16, g_mlp)


def _mlp_kernel(h_ref, x1_ref, wu_ref, wd_ref, gf_ref, o_ref):
    j = pl.program_id(1)
    last_j = pl.num_programs(1) - 1
    tm = o_ref.shape[0]
    piece = x1_ref.shape[0]
    n_chunks = tm // MLP_MC

    def step(first, last):
        wu = wu_ref[...].astype(BF16)
        wd = wd_ref[...].astype(BF16)

        def up(c):
            u = jnp.dot(h_ref[c * MLP_MC:(c + 1) * MLP_MC, :], wu, preferred_element_type=F32)
            r = jnp.maximum(u, 0.0)
            return (r * r).astype(BF16)

        def down(c, act):
            rows = slice(c * MLP_MC, (c + 1) * MLP_MC)
            y = jnp.dot(act, wd, preferred_element_type=F32)
            if not first:
                y = y + o_ref[rows, :]
            if last:
                y = y * _rms_scale(y) * gf_ref[...]
            o_ref[rows, :] = y

        if last:
            o_ref[tm - piece:tm, :] += x1_ref[...]
        act = up(0)
        for c in range(n_chunks):
            nxt = up(c + 1) if c + 1 < n_chunks else None
            down(c, act)
            act = nxt
        if first:
            o_ref[0:piece, :] += x1_ref[...]
        elif not last:
            o_ref[pl.ds(pl.multiple_of(j * piece, piece), piece), :] += x1_ref[...]

    pl.when(j == 0)(functools.partial(step, True, False))
    pl.when((j > 0) & (j < last_j))(functools.partial(step, False, False))
    pl.when(j == last_j)(functools.partial(step, False, True))


def _mlp(h, x1, w_up_f32, w_down_f32, g_final):
    rows = x1.shape[0]
    tm, tf = MLP_TM, MLP_TF
    steps = D_FF // tf
    piece = tm // steps
    return pl.pallas_call(
        _mlp_kernel,
        grid=(rows // tm, steps),
        in_specs=[
            pl.BlockSpec((tm, D_MODEL), lambda i, j: (i, 0)),
            pl.BlockSpec((piece, D_MODEL), lambda i, j: (i * steps + j, 0)),
            pl.BlockSpec((D_MODEL, tf), lambda i, j: (0, j)),
            pl.BlockSpec((tf, D_MODEL), lambda i, j: (j, 0)),
            pl.BlockSpec((1, D_MODEL), lambda i, j: (0, 0)),
        ],
        out_specs=pl.BlockSpec((tm, D_MODEL), lambda i, j: (i, 0)),
        out_shape=jax.ShapeDtypeStruct((rows, D_MODEL), F32),
        compiler_params=_compiler_params(("arbitrary", "arbitrary")),
        name="mlp",
    )(h, x1, w_up_f32, w_down_f32, g_final)


def kernel(x, norm_mix_g, w_in, lambda_q1, lambda_k1, lambda_q2, lambda_k2, diff_subln_g, na_rel_bias, w_out,
           norm_mlp_g, w_up, w_down, norm_final_g):
    batch, seq, d_model = x.shape
    assert d_model == D_MODEL and w_in.shape == (1, D_MODEL, N_GROUPS * GROUP_WIDTH)
    assert seq % PROJ_TM == 0 and seq % ATTN_TQ == 0 and seq % ATTN_KC == 0 and seq % (GRID_W * WIN_H) == 0
    x2 = x.reshape(batch * seq, D_MODEL)

    proj = _in_proj(x2, norm_mix_g, w_in[0], seq)
    a_out = _diff_attn(proj, lambda_q1, lambda_k1, lambda_q2, lambda_k2, diff_subln_g, batch, seq)
    n_out = _na_attn(proj, na_rel_bias[0].reshape(-1), batch, seq)
    x1, h_mlp = _out_proj(x2, a_out, n_out, w_out[0].astype(BF16), norm_mlp_g)
    y = _mlp(h_mlp, x1, w_up[0], w_down[0], norm_final_g.reshape(1, D_MODEL))
    return y.reshape(batch, seq, D_MODEL)
```

```python
import functools
import math

import jax
import jax.numpy as jnp
import numpy as np
from jax import lax
from jax.experimental import pallas as pl
from jax.experimental.pallas import tpu as pltpu

F32 = jnp.float32
BF16 = jnp.bfloat16

D_MODEL = 2048
HEAD_DIM = 128
N_HEADS = 8
GROUP_WIDTH = N_HEADS * HEAD_DIM
N_GROUPS = 6
DIFF_HALF = HEAD_DIM // 2
ROT_DIM = DIFF_HALF // 4
ROT_HALF = ROT_DIM // 2
ROPE_THETA = 500000.0
GRID_W = 64
WIN_H = 8
WIN_W = 16
D_FF = 4 * D_MODEL
EPS = 1e-5
NEG_INF = -1e30
LAMBDA_INIT = 0.8 - 0.6 * math.exp(-0.3 * 0)

LANES = 128
SUBLANES = 8
VMEM_LIMIT_BYTES = 56 * 1024 * 1024

PROJ_TM = 1024
PROJ_MC = 256
ATTN_TQ = 256
ATTN_KC = 512
OUT_TM = 512
OUT_MC = 128
MLP_TM = 1024
MLP_TF = 512
MLP_MC = 256
NA_ROWS_PER_GROUP = 16


def _compiler_params(semantics):
    return pltpu.CompilerParams(dimension_semantics=semantics, vmem_limit_bytes=VMEM_LIMIT_BYTES)


def _row_chunks(total, chunk):
    bounds = list(range(0, total, chunk)) + [total]
    bounds.insert(-1, bounds[-2] + chunk // 2)
    return list(zip(bounds[:-1], bounds[1:]))


def _rms_scale(x):
    return lax.rsqrt(jnp.mean(x * x, axis=-1, keepdims=True) + EPS)


LOG2_E = math.log2(math.e)
SCORE_SCALE_LOG2 = LOG2_E / math.sqrt(DIFF_HALF)
NA_SCALE_LOG2 = LOG2_E / math.sqrt(HEAD_DIM)


def _in_proj_kernel(x_ref, g_ref, w_ref, c_ref, s_lo_ref, s_hi_ref, o_ref, h_ref):
    j = pl.program_id(1)

    def step(first):
        has_rotary = True if first else j < 2
        q_scale = SCORE_SCALE_LOG2 if first else jnp.where(j == 3, NA_SCALE_LOG2, 1.0).astype(F32)
        w = w_ref[...].astype(BF16)
        for start, stop in _row_chunks(x_ref.shape[0], PROJ_MC):
            rows = slice(start, stop)
            if first:
                x = x_ref[rows, :]
                h_ref[rows, :] = (x * _rms_scale(x) * g_ref[...]).astype(BF16)
            acc = jnp.dot(h_ref[rows, :], w, preferred_element_type=F32)
            c = jnp.where(has_rotary, c_ref[rows, :], 1.0) * q_scale
            s_lo = jnp.where(has_rotary, s_lo_ref[rows, :], 0.0) * q_scale
            s_hi = jnp.where(has_rotary, s_hi_ref[rows, :], 0.0) * q_scale
            for hh in range(N_HEADS):
                a = acc[:, hh * HEAD_DIM:(hh + 1) * HEAD_DIM]
                r = a * c + pltpu.roll(a, LANES - ROT_HALF, 1) * s_lo + pltpu.roll(a, ROT_HALF, 1) * s_hi
                o_ref[rows, hh * HEAD_DIM:(hh + 1) * HEAD_DIM] = r.astype(BF16)

    pl.when(j == 0)(functools.partial(step, True))
    pl.when(j > 0)(functools.partial(step, False))


@functools.lru_cache(maxsize=None)
def _rotary_lane_tables(seq):
    lane = np.arange(LANES)
    inv_freq = np.power(np.float32(ROPE_THETA), -np.arange(0, ROT_DIM, 2, dtype=np.float32) / np.float32(ROT_DIM))
    ang = (np.arange(seq, dtype=np.float32)[:, None] * inv_freq[lane % ROT_HALF][None, :]).astype(np.float32)
    in_lo = (lane % DIFF_HALF) < ROT_HALF
    in_hi = ((lane % DIFF_HALF) >= ROT_HALF) & ((lane % DIFF_HALF) < ROT_DIM)
    cos, sin = np.cos(ang).astype(np.float32), np.sin(ang).astype(np.float32)
    c = np.where(in_lo | in_hi, cos, np.float32(1.0))
    s_lo = np.where(in_lo, -sin, np.float32(0.0))
    s_hi = np.where(in_hi, sin, np.float32(0.0))
    return c, s_lo, s_hi


def _in_proj(x2, g, w_f32, seq):
    rows = x2.shape[0]
    tm = PROJ_TM
    tables = [jnp.asarray(t) for t in _rotary_lane_tables(seq)]
    pos_blocks = seq // tm
    tab_spec = pl.BlockSpec((tm, LANES), lambda i, j: (i % pos_blocks, 0))
    return pl.pallas_call(
        _in_proj_kernel,
        grid=(rows // tm, N_GROUPS),
        in_specs=[
            pl.BlockSpec((tm, D_MODEL), lambda i, j: (i, 0)),
            pl.BlockSpec((1, D_MODEL), lambda i, j: (0, 0)),
            pl.BlockSpec((D_MODEL, GROUP_WIDTH), lambda i, j: (0, j)),
            tab_spec, tab_spec, tab_spec,
        ],
        out_specs=pl.BlockSpec((tm, GROUP_WIDTH), lambda i, j: (i, j)),
        out_shape=jax.ShapeDtypeStruct((rows, N_GROUPS * GROUP_WIDTH), BF16),
        scratch_shapes=[pltpu.VMEM((tm, D_MODEL), BF16)],
        compiler_params=_compiler_params(("arbitrary", "arbitrary")),
        name="in_proj",
    )(x2, g, w_f32, *tables)


def _diff_attn_kernel(lq1_ref, lk1_ref, lq2_ref, lk2_ref, g_ref, q_ref, k_ref, v_ref, o_ref, vt_ref, s_ref):
    seq = k_ref.shape[0]
    tq = ATTN_TQ
    n_tiles = seq // tq
    n_chunks = seq // ATTN_KC
    fold = ATTN_KC // SUBLANES
    assert n_tiles % 2 == 0 and n_tiles >= 4

    vt_ref[...] = v_ref[...].T
    lam = (jnp.exp(jnp.sum(lq1_ref[...] * lk1_ref[...])) - jnp.exp(jnp.sum(lq2_ref[...] * lk2_ref[...]))
           + LAMBDA_INIT)
    comp1_lane = lax.broadcasted_iota(jnp.int32, (tq, HEAD_DIM), 1) < DIFF_HALF

    def load_q_parts(t):
        q = q_ref[pl.ds(pl.multiple_of(t * tq, tq), tq), :]
        zero = jnp.zeros_like(q)
        return jnp.where(comp1_lane, q, zero), jnp.where(comp1_lane, zero, q)

    def score_chunk(slot, c, q_parts, m8):
        ks = k_ref[c * ATTN_KC:(c + 1) * ATTN_KC, :]
        new_m8 = []
        for comp in range(2):
            s = lax.dot_general(ks, q_parts[comp], (((1,), (1,)), ((), ())), preferred_element_type=F32)
            s_ref[slot, comp, c * ATTN_KC:(c + 1) * ATTN_KC, :] = s
            cm = jnp.max(s.reshape(fold, SUBLANES, tq), axis=0)
            new_m8.append(cm if m8 is None else jnp.maximum(m8[comp], cm))
        return new_m8

    def prob_chunk(slot, c, m, acc):
        vts = vt_ref[:, c * ATTN_KC:(c + 1) * ATTN_KC]
        new_acc = []
        for comp in range(2):
            l8, o = acc[comp]
            p = jnp.exp2(s_ref[slot, comp, c * ATTN_KC:(c + 1) * ATTN_KC, :] - m[comp])
            l8 = l8 + jnp.sum(p.reshape(fold, SUBLANES, tq), axis=0)
            o = o + jnp.dot(vts, p.astype(BF16), preferred_element_type=F32)
            new_acc.append((l8, o))
        return new_acc

    def col_max(m8):
        return [jnp.max(x, axis=0, keepdims=True) for x in m8]

    def zero_acc():
        return [(jnp.zeros((SUBLANES, tq), F32), jnp.zeros((HEAD_DIM, tq), F32)) for _ in range(2)]

    def finish(t, acc):
        outs = [o / jnp.sum(l8, axis=0, keepdims=True) for l8, o in acc]
        out = (outs[0] - lam * outs[1]).T
        y = out * _rms_scale(out) * g_ref[...]
        o_ref[pl.ds(pl.multiple_of(t * tq, tq), tq), :] = (y * (1.0 - LAMBDA_INIT)).astype(o_ref.dtype)

    def stage(t_next, t_cur, m_cur, t_prev, acc_prev):
        q_parts = load_q_parts(t_next[0]) if t_next is not None else None
        m8, acc = None, zero_acc()
        for c in range(n_chunks):
            if t_next is not None:
                m8 = score_chunk(t_next[1], c, q_parts, m8)
            if t_cur is not None:
                acc = prob_chunk(t_cur[1], c, m_cur, acc)
            if c == 0 and t_prev is not None:
                finish(t_prev, acc_prev)
        return (col_max(m8) if m8 is not None else None), acc

    m_cur, _ = stage((0, 0), None, None, None, None)
    m_cur, acc = stage((1, 1), (0, 0), m_cur, None, None)

    def pair(u, carry):
        m_odd, acc_even = carry
        m_even, acc_odd = stage((2 * u, 0), (2 * u - 1, 1), m_odd, 2 * u - 2, acc_even)
        m_odd, acc_even = stage((2 * u + 1, 1), (2 * u, 0), m_even, 2 * u - 1, acc_odd)
        return m_odd, acc_even

    m_cur, acc = lax.fori_loop(1, n_tiles // 2, pair, (m_cur, acc))
    _, acc_last = stage(None, (n_tiles - 1, 1), m_cur, n_tiles - 2, acc)
    finish(n_tiles - 1, acc_last)


def _diff_attn(proj, lq1, lk1, lq2, lk2, subln_g, batch, seq):
    lam_spec = pl.BlockSpec((1, DIFF_HALF), lambda b, h: (0, 0))
    head_blk = lambda col0: pl.BlockSpec((seq, HEAD_DIM), lambda b, h: (b, col0 + h))
    return pl.pallas_call(
        _diff_attn_kernel,
        grid=(batch, N_HEADS),
        in_specs=[
            lam_spec, lam_spec, lam_spec, lam_spec,
            pl.BlockSpec((1, HEAD_DIM), lambda b, h: (0, 0)),
            head_blk(0), head_blk(N_HEADS), head_blk(2 * N_HEADS),
        ],
        out_specs=pl.BlockSpec((seq, HEAD_DIM), lambda b, h: (b, h)),
        out_shape=jax.ShapeDtypeStruct((batch * seq, GROUP_WIDTH), BF16),
        scratch_shapes=[pltpu.VMEM((HEAD_DIM, seq), BF16), pltpu.VMEM((2, 2, seq, ATTN_TQ), F32)],
        compiler_params=_compiler_params(("arbitrary", "arbitrary")),
        name="diff_attn",
    )(lq1, lk1, lq2, lk2, subln_g, proj, proj, proj)


N_REL_ROWS = 2 * WIN_H - 1
N_REL_COLS = 2 * WIN_W - 1
N_PAIR_TABLES = N_REL_ROWS - 1


def _na_kernel(rb_ref, q_ref, k_ref, v_ref, o_ref, tab_ref):
    head = pl.program_id(0)
    rows = q_ref.shape[0] // GRID_W
    band = WIN_H * GRID_W

    @pl.when(pl.program_id(1) == 0)
    def _():
        qc = lax.broadcasted_iota(jnp.int32, (GRID_W, LANES), 0)
        ln = lax.broadcasted_iota(jnp.int32, (GRID_W, LANES), 1)
        kc = ln & (GRID_W - 1)
        upper = ln >= GRID_W
        rel = jnp.clip(kc - qc + (WIN_W - 1), 0, N_REL_COLS - 1)
        col_start = jnp.clip(qc - WIN_W // 2, 0, GRID_W - WIN_W)
        inside = (kc >= col_start) & (kc < col_start + WIN_W)
        base = head * (N_REL_ROWS * N_REL_COLS)

        def build(t, carry):
            acc = jnp.zeros((GRID_W, LANES), F32)
            for d in range(N_REL_COLS):
                lo = rb_ref[base + t * N_REL_COLS + d]
                hi = rb_ref[base + (t + 1) * N_REL_COLS + d]
                acc = jnp.where(rel == d, jnp.where(upper, hi, lo), acc)
            tab_ref[t] = jnp.where(inside, acc * LOG2_E, NEG_INF)
            return carry

        lax.fori_loop(0, N_PAIR_TABLES, build, 0)

    def band_start(r):
        return min(max(r - WIN_H // 2, 0), rows - WIN_H)

    def score(r):
        q = q_ref[r * GRID_W:(r + 1) * GRID_W, :]
        kb = k_ref[band_start(r) * GRID_W:band_start(r) * GRID_W + band, :]
        return lax.dot_general(q, kb, (((1,), (1,)), ((), ())), preferred_element_type=F32)

    def softmax(r, s):
        rel0 = band_start(r) - r + (WIN_H - 1)
        parts = [s[:, jj * LANES:(jj + 1) * LANES] + tab_ref[rel0 + 2 * jj] for jj in range(band // LANES)]
        logits = jnp.concatenate(parts, axis=-1)
        p = jnp.exp2(logits - jnp.max(logits, axis=-1, keepdims=True))
        return p.astype(BF16), jnp.sum(p, axis=-1, keepdims=True)

    def value(r, p, l):
        vb = v_ref[band_start(r) * GRID_W:band_start(r) * GRID_W + band, :]
        o_ref[r * GRID_W:(r + 1) * GRID_W, :] = (jnp.dot(p, vb, preferred_element_type=F32) / l).astype(o_ref.dtype)

    groups = [range(g, g + NA_ROWS_PER_GROUP) for g in range(0, rows, NA_ROWS_PER_GROUP)]
    scores = {r: score(r) for r in groups[0]}
    for gi, group in enumerate(groups):
        probs = {r: softmax(r, scores.pop(r)) for r in group}
        nxt = groups[gi + 1] if gi + 1 < len(groups) else ()
        for u, r in enumerate(group):
            if u < len(nxt):
                scores[nxt[u]] = score(nxt[u])
            value(r, *probs[r])


def _na_attn(proj, rel_bias_flat, batch, seq):
    blk = lambda col0: pl.BlockSpec((seq, HEAD_DIM), lambda h, b: (b, col0 + h))
    return pl.pallas_call(
        _na_kernel,
        grid=(N_HEADS, batch),
        in_specs=[
            pl.BlockSpec(memory_space=pltpu.SMEM),
            blk(3 * N_HEADS), blk(4 * N_HEADS), blk(5 * N_HEADS),
        ],
        out_specs=pl.BlockSpec((seq, HEAD_DIM), lambda h, b: (b, h)),
        out_shape=jax.ShapeDtypeStruct((batch * seq, GROUP_WIDTH), BF16),
        scratch_shapes=[pltpu.VMEM((N_PAIR_TABLES, GRID_W, LANES), F32)],
        compiler_params=_compiler_params(("arbitrary", "arbitrary")),
        name="na_attn",
    )(rel_bias_flat, proj, proj, proj)


def _out_proj_kernel(x_ref, a_ref, n_ref, wa_ref, wn_ref, g_ref, x1_ref, h_ref):
    for mc in range(x_ref.shape[0] // OUT_MC):
        rows = slice(mc * OUT_MC, (mc + 1) * OUT_MC)
        acc = jnp.dot(a_ref[rows, :], wa_ref[...], preferred_element_type=F32)
        acc = acc + jnp.dot(n_ref[rows, :], wn_ref[...], preferred_element_type=F32)
        x1 = x_ref[rows, :] + acc
        x1_ref[rows, :] = x1
        h_ref[rows, :] = (x1 * _rms_scale(x1) * g_ref[...]).astype(BF16)


def _out_proj(x2, a_out, n_out, w_out_bf16, g_mlp):
    rows = x2.shape[0]
    tm = OUT_TM
    row_blk = lambda width: pl.BlockSpec((tm, width), lambda i: (i, 0))
    return pl.pallas_call(
        _out_proj_kernel,
        grid=(rows // tm,),
        in_specs=[
            row_blk(D_MODEL), row_blk(GROUP_WIDTH), row_blk(GROUP_WIDTH),
            pl.BlockSpec((GROUP_WIDTH, D_MODEL), lambda i: (0, 0)),
            pl.BlockSpec((GROUP_WIDTH, D_MODEL), lambda i: (1, 0)),
            pl.BlockSpec((1, D_MODEL), lambda i: (0, 0)),
        ],
        out_specs=[row_blk(D_MODEL), row_blk(D_MODEL)],
        out_shape=[jax.ShapeDtypeStruct((rows, D_MODEL), F32), jax.ShapeDtypeStruct((rows, D_MODEL), BF16)],
        compiler_params=_compiler_params(("arbitrary",)),
        name="out_proj",
    )(x2, a_out, n_out, w_out_bf16, w_out_bf16, g_mlp)


def _mlp_kernel(h_ref, x1_ref, wu_ref, wd_ref, gf_ref, o_ref):
    j = pl.program_id(1)
    last_j = pl.num_programs(1) - 1
    tm = o_ref.shape[0]
    piece = x1_ref.shape[0]
    chunks = [(start, start + MLP_MC) for start in range(0, tm, MLP_MC)]
    n_chunks = len(chunks)

    def step(first, last):
        wu = wu_ref[...].astype(BF16)
        wd = wd_ref[...].astype(BF16)

        def up(c):
            u = jnp.dot(h_ref[slice(*chunks[c]), :], wu, preferred_element_type=F32)
            r = jnp.maximum(u, 0.0)
            return (r * r).astype(BF16)

        def down(c, act):
            rows = slice(*chunks[c])
            y = jnp.dot(act, wd, preferred_element_type=F32)
            if not first:
                y = y + o_ref[rows, :]
            if last:
                y = y * _rms_scale(y) * gf_ref[...]
            o_ref[rows, :] = y

        if last:
            o_ref[tm - piece:tm, :] += x1_ref[...]
        act = up(0)
        for c in range(n_chunks):
            nxt = up(c + 1) if c + 1 < n_chunks else None
            down(c, act)
            act = nxt
        if first:
            o_ref[0:piece, :] += x1_ref[...]
        elif not last:
            o_ref[pl.ds(pl.multiple_of(j * piece, piece), piece), :] += x1_ref[...]

    pl.when(j == 0)(functools.partial(step, True, False))
    pl.when((j > 0) & (j < last_j))(functools.partial(step, False, False))
    pl.when(j == last_j)(functools.partial(step, False, True))


def _mlp(h, x1, w_up_f32, w_down_f32, g_final):
    rows = x1.shape[0]
    tm, tf = MLP_TM, MLP_TF
    steps = D_FF // tf
    piece = tm // steps
    return pl.pallas_call(
        _mlp_kernel,
        grid=(rows // tm, steps),
        in_specs=[
            pl.BlockSpec((tm, D_MODEL), lambda i, j: (i, 0)),
            pl.BlockSpec((piece, D_MODEL), lambda i, j: (i * steps + j, 0)),
            pl.BlockSpec((D_MODEL, tf), lambda i, j: (0, j)),
            pl.BlockSpec((tf, D_MODEL), lambda i, j: (j, 0)),
            pl.BlockSpec((1, D_MODEL), lambda i, j: (0, 0)),
        ],
        out_specs=pl.BlockSpec((tm, D_MODEL), lambda i, j: (i, 0)),
        out_shape=jax.ShapeDtypeStruct((rows, D_MODEL), F32),
        compiler_params=_compiler_params(("arbitrary", "arbitrary")),
        name="mlp",
    )(h, x1, w_up_f32, w_down_f32, g_final)


def kernel(x, norm_mix_g, w_in, lambda_q1, lambda_k1, lambda_q2, lambda_k2, diff_subln_g, na_rel_bias, w_out,
           norm_mlp_g, w_up, w_down, norm_final_g):
    batch, seq, d_model = x.shape
    assert d_model == D_MODEL and w_in.shape == (1, D_MODEL, N_GROUPS * GROUP_WIDTH)
    assert seq % PROJ_TM == 0 and seq % ATTN_TQ == 0 and seq % ATTN_KC == 0 and seq % (GRID_W * WIN_H) == 0
    x2 = x.reshape(batch * seq, D_MODEL)

    proj = _in_proj(x2, norm_mix_g, w_in[0], seq)
    a_out = _diff_attn(proj, lambda_q1, lambda_k1, lambda_q2, lambda_k2, diff_subln_g, batch, seq)
    n_out = _na_attn(proj, na_rel_bias[0].reshape(-1), batch, seq)
    x1, h_mlp = _out_proj(x2, a_out, n_out, w_out[0].astype(BF16), norm_mlp_g)
    y = _mlp(h_mlp, x1, w_up[0], w_down[0], norm_final_g.reshape(1, D_MODEL))
    return y.reshape(batch, seq, D_MODEL)
```

```python
import functools
import math

import jax
import jax.numpy as jnp
import numpy as np
from jax import lax
from jax.experimental import pallas as pl
from jax.experimental.pallas import tpu as pltpu

F32 = jnp.float32
BF16 = jnp.bfloat16

D_MODEL = 2048
HEAD_DIM = 128
N_HEADS = 8
GROUP_WIDTH = N_HEADS * HEAD_DIM
N_GROUPS = 6
GROUP_DQ, GROUP_DK, GROUP_DV, GROUP_NQ, GROUP_NK, GROUP_NV = range(N_GROUPS)
DIFF_HALF = HEAD_DIM // 2
ROT_DIM = DIFF_HALF // 4
ROT_HALF = ROT_DIM // 2
ROPE_THETA = 500000.0
GRID_W = 64
WIN_H = 8
WIN_W = 16
D_FF = 4 * D_MODEL
EPS = 1e-5
NEG_INF = -1e30
LAMBDA_INIT = 0.8 - 0.6 * math.exp(-0.3 * 0)

LANES = 128
SUBLANES = 8
VMEM_LIMIT_BYTES = 56 * 1024 * 1024

PROJ_TM = 1024
PROJ_MC = 256
ATTN_TQ = 256
ATTN_KC = 512
OUT_TM = 512
OUT_MC = 256
MLP_TM = 1024
MLP_TF = 512
MLP_MC = 512
NA_ROWS_PER_GROUP = 16


def _compiler_params(semantics):
    return pltpu.CompilerParams(dimension_semantics=semantics, vmem_limit_bytes=VMEM_LIMIT_BYTES)


def _row_chunks(total, chunk):
    bounds = list(range(0, total, chunk)) + [total]
    bounds.insert(-1, bounds[-2] + chunk // 2)
    return list(zip(bounds[:-1], bounds[1:]))


def _rms_scale(x):
    return lax.rsqrt(jnp.mean(x * x, axis=-1, keepdims=True) + EPS)


LOG2_E = math.log2(math.e)
SCORE_SCALE_LOG2 = LOG2_E / math.sqrt(DIFF_HALF)
NA_SCALE_LOG2 = LOG2_E / math.sqrt(HEAD_DIM)


def _in_proj_kernel(x_ref, g_ref, w_ref, c_ref, s_lo_ref, s_hi_ref, o_ref, h_ref):
    j = pl.program_id(1)

    def step(first):
        has_rotary = True if first else j == GROUP_DK
        q_scale = SCORE_SCALE_LOG2 if first else jnp.where(j == GROUP_NQ, NA_SCALE_LOG2, 1.0).astype(F32)
        w = w_ref[...].astype(BF16)
        for start, stop in _row_chunks(x_ref.shape[0], PROJ_MC):
            rows = slice(start, stop)
            if first:
                x = x_ref[rows, :]
                h_ref[rows, :] = (x * _rms_scale(x) * g_ref[...]).astype(BF16)
            acc = jnp.dot(h_ref[rows, :], w, preferred_element_type=F32)
            c = jnp.where(has_rotary, c_ref[rows, :], 1.0) * q_scale
            s_lo = jnp.where(has_rotary, s_lo_ref[rows, :], 0.0) * q_scale
            s_hi = jnp.where(has_rotary, s_hi_ref[rows, :], 0.0) * q_scale
            for hh in range(N_HEADS):
                a = acc[:, hh * HEAD_DIM:(hh + 1) * HEAD_DIM]
                r = a * c + pltpu.roll(a, LANES - ROT_HALF, 1) * s_lo + pltpu.roll(a, ROT_HALF, 1) * s_hi
                o_ref[rows, hh * HEAD_DIM:(hh + 1) * HEAD_DIM] = r.astype(BF16)

    assert GROUP_DQ == 0
    pl.when(j == GROUP_DQ)(functools.partial(step, True))
    pl.when(j != GROUP_DQ)(functools.partial(step, False))


@functools.lru_cache(maxsize=None)
def _rotary_lane_tables(seq):
    lane = np.arange(LANES)
    inv_freq = np.power(np.float32(ROPE_THETA), -np.arange(0, ROT_DIM, 2, dtype=np.float32) / np.float32(ROT_DIM))
    ang = (np.arange(seq, dtype=np.float32)[:, None] * inv_freq[lane % ROT_HALF][None, :]).astype(np.float32)
    in_lo = (lane % DIFF_HALF) < ROT_HALF
    in_hi = ((lane % DIFF_HALF) >= ROT_HALF) & ((lane % DIFF_HALF) < ROT_DIM)
    cos, sin = np.cos(ang).astype(np.float32), np.sin(ang).astype(np.float32)
    c = np.where(in_lo | in_hi, cos, np.float32(1.0))
    s_lo = np.where(in_lo, -sin, np.float32(0.0))
    s_hi = np.where(in_hi, sin, np.float32(0.0))
    return c, s_lo, s_hi


def _in_proj(x2, g, w_f32, seq):
    rows = x2.shape[0]
    tm = PROJ_TM
    tables = [jnp.asarray(t) for t in _rotary_lane_tables(seq)]
    pos_blocks = seq // tm
    tab_spec = pl.BlockSpec((tm, LANES), lambda i, j: (i % pos_blocks, 0))
    return pl.pallas_call(
        _in_proj_kernel,
        grid=(rows // tm, N_GROUPS),
        in_specs=[
            pl.BlockSpec((tm, D_MODEL), lambda i, j: (i, 0)),
            pl.BlockSpec((1, D_MODEL), lambda i, j: (0, 0)),
            pl.BlockSpec((D_MODEL, GROUP_WIDTH), lambda i, j: (0, j)),
            tab_spec, tab_spec, tab_spec,
        ],
        out_specs=pl.BlockSpec((tm, GROUP_WIDTH), lambda i, j: (i, j)),
        out_shape=jax.ShapeDtypeStruct((rows, N_GROUPS * GROUP_WIDTH), BF16),
        scratch_shapes=[pltpu.VMEM((tm, D_MODEL), BF16)],
        compiler_params=_compiler_params(("arbitrary", "arbitrary")),
        name="in_proj",
    )(x2, g, w_f32, *tables)


def _diff_attn_kernel(lq1_ref, lk1_ref, lq2_ref, lk2_ref, g_ref, q_ref, k_ref, v_ref, o_ref, vt_ref, s_ref):
    seq = k_ref.shape[0]
    tq = ATTN_TQ
    n_tiles = seq // tq
    n_chunks = seq // ATTN_KC
    fold = ATTN_KC // SUBLANES
    assert n_tiles % 2 == 0 and n_tiles >= 4

    vt_ref[...] = v_ref[...].T
    lam = (jnp.exp(jnp.sum(lq1_ref[...] * lk1_ref[...])) - jnp.exp(jnp.sum(lq2_ref[...] * lk2_ref[...]))
           + LAMBDA_INIT)
    comp1_lane = lax.broadcasted_iota(jnp.int32, (tq, HEAD_DIM), 1) < DIFF_HALF

    def load_q_parts(t):
        q = q_ref[pl.ds(pl.multiple_of(t * tq, tq), tq), :]
        zero = jnp.zeros_like(q)
        return jnp.where(comp1_lane, q, zero), jnp.where(comp1_lane, zero, q)

    def score_chunk(slot, c, q_parts, m8):
        ks = k_ref[c * ATTN_KC:(c + 1) * ATTN_KC, :]
        new_m8 = []
        for comp in range(2):
            s = lax.dot_general(ks, q_parts[comp], (((1,), (1,)), ((), ())), preferred_element_type=F32)
            s_ref[slot, comp, c * ATTN_KC:(c + 1) * ATTN_KC, :] = s
            cm = jnp.max(s.reshape(fold, SUBLANES, tq), axis=0)
            new_m8.append(cm if m8 is None else jnp.maximum(m8[comp], cm))
        return new_m8

    def prob_chunk(slot, c, m, acc):
        vts = vt_ref[:, c * ATTN_KC:(c + 1) * ATTN_KC]
        new_acc = []
        for comp in range(2):
            l8, o = acc[comp]
            p = jnp.exp2(s_ref[slot, comp, c * ATTN_KC:(c + 1) * ATTN_KC, :] - m[comp])
            l8 = l8 + jnp.sum(p.reshape(fold, SUBLANES, tq), axis=0)
            o = o + jnp.dot(vts, p.astype(BF16), preferred_element_type=F32)
            new_acc.append((l8, o))
        return new_acc

    def col_max(m8):
        return [jnp.max(x, axis=0, keepdims=True) for x in m8]

    def zero_acc():
        return [(jnp.zeros((SUBLANES, tq), F32), jnp.zeros((HEAD_DIM, tq), F32)) for _ in range(2)]

    def finish(t, acc):
        outs = [o / jnp.sum(l8, axis=0, keepdims=True) for l8, o in acc]
        out = (outs[0] - lam * outs[1]).T
        y = out * _rms_scale(out) * g_ref[...]
        o_ref[pl.ds(pl.multiple_of(t * tq, tq), tq), :] = (y * (1.0 - LAMBDA_INIT)).astype(o_ref.dtype)

    def stage(t_next, t_cur, m_cur, t_prev, acc_prev):
        q_parts = load_q_parts(t_next[0]) if t_next is not None else None
        m8, acc = None, zero_acc()
        for c in range(n_chunks):
            if t_next is not None:
                m8 = score_chunk(t_next[1], c, q_parts, m8)
            if t_cur is not None:
                acc = prob_chunk(t_cur[1], c, m_cur, acc)
            if c == 0 and t_prev is not None:
                finish(t_prev, acc_prev)
        return (col_max(m8) if m8 is not None else None), acc

    m_cur, _ = stage((0, 0), None, None, None, None)
    m_cur, acc = stage((1, 1), (0, 0), m_cur, None, None)

    def pair(u, carry):
        m_odd, acc_even = carry
        m_even, acc_odd = stage((2 * u, 0), (2 * u - 1, 1), m_odd, 2 * u - 2, acc_even)
        m_odd, acc_even = stage((2 * u + 1, 1), (2 * u, 0), m_even, 2 * u - 1, acc_odd)
        return m_odd, acc_even

    m_cur, acc = lax.fori_loop(1, n_tiles // 2, pair, (m_cur, acc))
    _, acc_last = stage(None, (n_tiles - 1, 1), m_cur, n_tiles - 2, acc)
    finish(n_tiles - 1, acc_last)


def _diff_attn(proj, lq1, lk1, lq2, lk2, subln_g, batch, seq):
    lam_spec = pl.BlockSpec((1, DIFF_HALF), lambda b, h: (0, 0))
    head_blk = lambda col0: pl.BlockSpec((seq, HEAD_DIM), lambda b, h: (b, col0 + h))
    return pl.pallas_call(
        _diff_attn_kernel,
        grid=(batch, N_HEADS),
        in_specs=[
            lam_spec, lam_spec, lam_spec, lam_spec,
            pl.BlockSpec((1, HEAD_DIM), lambda b, h: (0, 0)),
            head_blk(GROUP_DQ * N_HEADS), head_blk(GROUP_DK * N_HEADS), head_blk(GROUP_DV * N_HEADS),
        ],
        out_specs=pl.BlockSpec((seq, HEAD_DIM), lambda b, h: (b, h)),
        out_shape=jax.ShapeDtypeStruct((batch * seq, GROUP_WIDTH), BF16),
        scratch_shapes=[pltpu.VMEM((HEAD_DIM, seq), BF16), pltpu.VMEM((2, 2, seq, ATTN_TQ), F32)],
        compiler_params=_compiler_params(("arbitrary", "arbitrary")),
        name="diff_attn",
    )(lq1, lk1, lq2, lk2, subln_g, proj, proj, proj)


N_REL_ROWS = 2 * WIN_H - 1
N_REL_COLS = 2 * WIN_W - 1
N_PAIR_TABLES = N_REL_ROWS - 1


def _na_kernel(rb_ref, q_ref, k_ref, v_ref, o_ref, tab_ref):
    head = pl.program_id(0)
    rows = q_ref.shape[0] // GRID_W
    band = WIN_H * GRID_W

    @pl.when(pl.program_id(1) == 0)
    def _():
        qc = lax.broadcasted_iota(jnp.int32, (GRID_W, LANES), 0)
        ln = lax.broadcasted_iota(jnp.int32, (GRID_W, LANES), 1)
        kc = ln & (GRID_W - 1)
        upper = ln >= GRID_W
        rel = jnp.clip(kc - qc + (WIN_W - 1), 0, N_REL_COLS - 1)
        col_start = jnp.clip(qc - WIN_W // 2, 0, GRID_W - WIN_W)
        inside = (kc >= col_start) & (kc < col_start + WIN_W)
        base = head * (N_REL_ROWS * N_REL_COLS)

        def build(t, carry):
            acc = jnp.zeros((GRID_W, LANES), F32)
            for d in range(N_REL_COLS):
                lo = rb_ref[base + t * N_REL_COLS + d]
                hi = rb_ref[base + (t + 1) * N_REL_COLS + d]
                acc = jnp.where(rel == d, jnp.where(upper, hi, lo), acc)
            tab_ref[t] = jnp.where(inside, acc * LOG2_E, NEG_INF)
            return carry

        lax.fori_loop(0, N_PAIR_TABLES, build, 0)

    def band_start(r):
        return min(max(r - WIN_H // 2, 0), rows - WIN_H)

    def score(r):
        q = q_ref[r * GRID_W:(r + 1) * GRID_W, :]
        kb = k_ref[band_start(r) * GRID_W:band_start(r) * GRID_W + band, :]
        return lax.dot_general(q, kb, (((1,), (1,)), ((), ())), preferred_element_type=F32)

    def softmax(r, s):
        rel0 = band_start(r) - r + (WIN_H - 1)
        parts = [s[:, jj * LANES:(jj + 1) * LANES] + tab_ref[rel0 + 2 * jj] for jj in range(band // LANES)]
        logits = jnp.concatenate(parts, axis=-1)
        p = jnp.exp2(logits - jnp.max(logits, axis=-1, keepdims=True))
        return p.astype(BF16), jnp.sum(p, axis=-1, keepdims=True)

    def value(r, p, l):
        vb = v_ref[band_start(r) * GRID_W:band_start(r) * GRID_W + band, :]
        o_ref[r * GRID_W:(r + 1) * GRID_W, :] = (jnp.dot(p, vb, preferred_element_type=F32) / l).astype(o_ref.dtype)

    groups = [range(g, g + NA_ROWS_PER_GROUP) for g in range(0, rows, NA_ROWS_PER_GROUP)]
    scores = {r: score(r) for r in groups[0]}
    for gi, group in enumerate(groups):
        probs = {r: softmax(r, scores.pop(r)) for r in group}
        nxt = groups[gi + 1] if gi + 1 < len(groups) else ()
        for u, r in enumerate(group):
            if u < len(nxt):
                scores[nxt[u]] = score(nxt[u])
            value(r, *probs[r])


def _na_attn(proj, rel_bias_flat, batch, seq):
    blk = lambda col0: pl.BlockSpec((seq, HEAD_DIM), lambda h, b: (b, col0 + h))
    return pl.pallas_call(
        _na_kernel,
        grid=(N_HEADS, batch),
        in_specs=[
            pl.BlockSpec(memory_space=pltpu.SMEM),
            blk(GROUP_NQ * N_HEADS), blk(GROUP_NK * N_HEADS), blk(GROUP_NV * N_HEADS),
        ],
        out_specs=pl.BlockSpec((seq, HEAD_DIM), lambda h, b: (b, h)),
        out_shape=jax.ShapeDtypeStruct((batch * seq, GROUP_WIDTH), BF16),
        scratch_shapes=[pltpu.VMEM((N_PAIR_TABLES, GRID_W, LANES), F32)],
        compiler_params=_compiler_params(("arbitrary", "arbitrary")),
        name="na_attn",
    )(rel_bias_flat, proj, proj, proj)


def _out_proj_kernel(x_ref, a_ref, n_ref, wa_ref, wn_ref, g_ref, x1_ref, h_ref):
    for mc in range(x_ref.shape[0] // OUT_MC):
        rows = slice(mc * OUT_MC, (mc + 1) * OUT_MC)
        acc = jnp.dot(a_ref[rows, :], wa_ref[...], preferred_element_type=F32)
        acc = acc + jnp.dot(n_ref[rows, :], wn_ref[...], preferred_element_type=F32)
        x1 = x_ref[rows, :] + acc
        x1_ref[rows, :] = x1
        h_ref[rows, :] = (x1 * _rms_scale(x1) * g_ref[...]).astype(BF16)


def _out_proj(x2, a_out, n_out, w_out_bf16, g_mlp):
    rows = x2.shape[0]
    tm = OUT_TM
    row_blk = lambda width: pl.BlockSpec((tm, width), lambda i: (i, 0))
    return pl.pallas_call(
        _out_proj_kernel,
        grid=(rows // tm,),
        in_specs=[
            row_blk(D_MODEL), row_blk(GROUP_WIDTH), row_blk(GROUP_WIDTH),
            pl.BlockSpec((GROUP_WIDTH, D_MODEL), lambda i: (0, 0)),
            pl.BlockSpec((GROUP_WIDTH, D_MODEL), lambda i: (1, 0)),
            pl.BlockSpec((1, D_MODEL), lambda i: (0, 0)),
        ],
        out_specs=[row_blk(D_MODEL), row_blk(D_MODEL)],
        out_shape=[jax.ShapeDtypeStruct((rows, D_MODEL), F32), jax.ShapeDtypeStruct((rows, D_MODEL), BF16)],
        compiler_params=_compiler_params(("arbitrary",)),
        name="out_proj",
    )(x2, a_out, n_out, w_out_bf16, w_out_bf16, g_mlp)


def _mlp_kernel(h_ref, x1_ref, wu_ref, wd_ref, gf_ref, o_ref):
    j = pl.program_id(1)
    last_j = pl.num_programs(1) - 1
    tm = o_ref.shape[0]
    piece = x1_ref.shape[0]
    chunks = [(start, start + MLP_MC) for start in range(0, tm, MLP_MC)]
    n_chunks = len(chunks)

    def step(first, last):
        wu = wu_ref[...].astype(BF16)
        wd = wd_ref[...].astype(BF16)

        def up(c):
            u = jnp.dot(h_ref[slice(*chunks[c]), :], wu, preferred_element_type=F32)
            r = jnp.maximum(u, 0.0)
            return (r * r).astype(BF16)

        def down(c, act):
            rows = slice(*chunks[c])
            y = jnp.dot(act, wd, preferred_element_type=F32)
            if not first:
                y = y + o_ref[rows, :]
            if last:
                y = y * _rms_scale(y) * gf_ref[...]
            o_ref[rows, :] = y

        if last:
            o_ref[tm - piece:tm, :] += x1_ref[...]
        act = up(0)
        for c in range(n_chunks):
            nxt = up(c + 1) if c + 1 < n_chunks else None
            down(c, act)
            act = nxt
        if first:
            o_ref[0:piece, :] += x1_ref[...]
        elif not last:
            o_ref[pl.ds(pl.multiple_of(j * piece, piece), piece), :] += x1_ref[...]

    pl.when(j == 0)(functools.partial(step, True, False))
    pl.when((j > 0) & (j < last_j))(functools.partial(step, False, False))
    pl.when(j == last_j)(functools.partial(step, False, True))


def _mlp(h, x1, w_up_f32, w_down_f32, g_final):
    rows = x1.shape[0]
    tm, tf = MLP_TM, MLP_TF
    steps = D_FF // tf
    piece = tm // steps
    return pl.pallas_call(
        _mlp_kernel,
        grid=(rows // tm, steps),
        in_specs=[
            pl.BlockSpec((tm, D_MODEL), lambda i, j: (i, 0)),
            pl.BlockSpec((piece, D_MODEL), lambda i, j: (i * steps + j, 0)),
            pl.BlockSpec((D_MODEL, tf), lambda i, j: (0, j)),
            pl.BlockSpec((tf, D_MODEL), lambda i, j: (j, 0)),
            pl.BlockSpec((1, D_MODEL), lambda i, j: (0, 0)),
        ],
        out_specs=pl.BlockSpec((tm, D_MODEL), lambda i, j: (i, 0)),
        out_shape=jax.ShapeDtypeStruct((rows, D_MODEL), F32),
        compiler_params=_compiler_params(("arbitrary", "arbitrary")),
        name="mlp",
    )(h, x1, w_up_f32, w_down_f32, g_final)


def kernel(x, norm_mix_g, w_in, lambda_q1, lambda_k1, lambda_q2, lambda_k2, diff_subln_g, na_rel_bias, w_out,
           norm_mlp_g, w_up, w_down, norm_final_g):
    batch, seq, d_model = x.shape
    assert d_model == D_MODEL and w_in.shape == (1, D_MODEL, N_GROUPS * GROUP_WIDTH)
    assert seq % PROJ_TM == 0 and seq % ATTN_TQ == 0 and seq % ATTN_KC == 0 and seq % (GRID_W * WIN_H) == 0
    x2 = x.reshape(batch * seq, D_MODEL)

    proj = _in_proj(x2, norm_mix_g, w_in[0], seq)
    a_out = _diff_attn(proj, lambda_q1, lambda_k1, lambda_q2, lambda_k2, diff_subln_g, batch, seq)
    n_out = _na_attn(proj, na_rel_bias[0].reshape(-1), batch, seq)
    x1, h_mlp = _out_proj(x2, a_out, n_out, w_out[0].astype(BF16), norm_mlp_g)
    y = _mlp(h_mlp, x1, w_up[0], w_down[0], norm_final_g.reshape(1, D_MODEL))
    return y.reshape(batch, seq, D_MODEL)
```

```python
import functools
import math

import jax
import jax.numpy as jnp
import numpy as np
from jax import lax
from jax.experimental import pallas as pl
from jax.experimental.pallas import tpu as pltpu

F32 = jnp.float32
BF16 = jnp.bfloat16

D_MODEL = 2048
HEAD_DIM = 128
N_HEADS = 8
GROUP_WIDTH = N_HEADS * HEAD_DIM
N_GROUPS = 6
GROUP_DQ, GROUP_DK, GROUP_DV, GROUP_NQ, GROUP_NK, GROUP_NV = range(N_GROUPS)
DIFF_HALF = HEAD_DIM // 2
ROT_DIM = DIFF_HALF // 4
ROT_HALF = ROT_DIM // 2
ROPE_THETA = 500000.0
GRID_W = 64
WIN_H = 8
WIN_W = 16
D_FF = 4 * D_MODEL
EPS = 1e-5
NEG_INF = -1e30
LAMBDA_INIT = 0.8 - 0.6 * math.exp(-0.3 * 0)

LANES = 128
SUBLANES = 8
VMEM_LIMIT_BYTES = 56 * 1024 * 1024

PROJ_TM = 1024
PROJ_MC = 256
ATTN_TQ = 256
ATTN_KC = 512
OUT_TM = 512
OUT_MC = 256
MLP_TM = 1024
MLP_TF = 512
MLP_MC = 512
NA_ROWS_PER_GROUP = 16


def _compiler_params(semantics):
    return pltpu.CompilerParams(dimension_semantics=semantics, vmem_limit_bytes=VMEM_LIMIT_BYTES)


def _row_chunks(total, chunk):
    bounds = list(range(0, total, chunk)) + [total]
    bounds.insert(-1, bounds[-2] + chunk // 2)
    return list(zip(bounds[:-1], bounds[1:]))


def _rms_scale(x):
    return lax.rsqrt(jnp.mean(x * x, axis=-1, keepdims=True) + EPS)


LOG2_E = math.log2(math.e)
SCORE_SCALE_LOG2 = LOG2_E / math.sqrt(DIFF_HALF)
NA_SCALE_LOG2 = LOG2_E / math.sqrt(HEAD_DIM)


def _in_proj_kernel(x_ref, g_ref, w_ref, c_ref, s_lo_ref, s_hi_ref, o_ref, h_ref):
    j = pl.program_id(1)

    def step(first, rotary):
        if rotary:
            q_scale = SCORE_SCALE_LOG2 if first else 1.0
        else:
            q_scale = jnp.where(j == GROUP_NQ, NA_SCALE_LOG2, 1.0).astype(F32)
        w = w_ref[...].astype(BF16)
        for start, stop in _row_chunks(x_ref.shape[0], PROJ_MC):
            rows = slice(start, stop)
            if first:
                x = x_ref[rows, :]
                h_ref[rows, :] = (x * _rms_scale(x) * g_ref[...]).astype(BF16)
            acc = jnp.dot(h_ref[rows, :], w, preferred_element_type=F32)
            if not rotary:
                o_ref[rows, :] = (acc * q_scale).astype(BF16)
                continue
            c = c_ref[rows, :] * q_scale
            s_lo = s_lo_ref[rows, :] * q_scale
            s_hi = s_hi_ref[rows, :] * q_scale
            for hh in range(N_HEADS):
                a = acc[:, hh * HEAD_DIM:(hh + 1) * HEAD_DIM]
                r = a * c + pltpu.roll(a, LANES - ROT_HALF, 1) * s_lo + pltpu.roll(a, ROT_HALF, 1) * s_hi
                o_ref[rows, hh * HEAD_DIM:(hh + 1) * HEAD_DIM] = r.astype(BF16)

    assert GROUP_DQ == 0
    pl.when(j == GROUP_DQ)(functools.partial(step, True, True))
    pl.when(j == GROUP_DK)(functools.partial(step, False, True))
    pl.when((j != GROUP_DQ) & (j != GROUP_DK))(functools.partial(step, False, False))


@functools.lru_cache(maxsize=None)
def _rotary_lane_tables(seq):
    lane = np.arange(LANES)
    inv_freq = np.power(np.float32(ROPE_THETA), -np.arange(0, ROT_DIM, 2, dtype=np.float32) / np.float32(ROT_DIM))
    ang = (np.arange(seq, dtype=np.float32)[:, None] * inv_freq[lane % ROT_HALF][None, :]).astype(np.float32)
    in_lo = (lane % DIFF_HALF) < ROT_HALF
    in_hi = ((lane % DIFF_HALF) >= ROT_HALF) & ((lane % DIFF_HALF) < ROT_DIM)
    cos, sin = np.cos(ang).astype(np.float32), np.sin(ang).astype(np.float32)
    c = np.where(in_lo | in_hi, cos, np.float32(1.0))
    s_lo = np.where(in_lo, -sin, np.float32(0.0))
    s_hi = np.where(in_hi, sin, np.float32(0.0))
    return c, s_lo, s_hi


def _in_proj(x2, g, w_f32, seq):
    rows = x2.shape[0]
    tm = PROJ_TM
    tables = [jnp.asarray(t) for t in _rotary_lane_tables(seq)]
    pos_blocks = seq // tm
    tab_spec = pl.BlockSpec((tm, LANES), lambda i, j: (i % pos_blocks, 0))
    return pl.pallas_call(
        _in_proj_kernel,
        grid=(rows // tm, N_GROUPS),
        in_specs=[
            pl.BlockSpec((tm, D_MODEL), lambda i, j: (i, 0)),
            pl.BlockSpec((1, D_MODEL), lambda i, j: (0, 0)),
            pl.BlockSpec((D_MODEL, GROUP_WIDTH), lambda i, j: (0, j)),
            tab_spec, tab_spec, tab_spec,
        ],
        out_specs=pl.BlockSpec((tm, GROUP_WIDTH), lambda i, j: (i, j)),
        out_shape=jax.ShapeDtypeStruct((rows, N_GROUPS * GROUP_WIDTH), BF16),
        scratch_shapes=[pltpu.VMEM((tm, D_MODEL), BF16)],
        compiler_params=_compiler_params(("arbitrary", "arbitrary")),
        name="in_proj",
    )(x2, g, w_f32, *tables)


def _diff_attn_kernel(lq1_ref, lk1_ref, lq2_ref, lk2_ref, g_ref, q_ref, k_ref, v_ref, o_ref, vt_ref, s_ref):
    seq = k_ref.shape[0]
    tq = ATTN_TQ
    n_tiles = seq // tq
    n_chunks = seq // ATTN_KC
    fold = ATTN_KC // SUBLANES
    assert n_tiles % 2 == 0 and n_tiles >= 4

    vt_ref[...] = v_ref[...].T
    lam = (jnp.exp(jnp.sum(lq1_ref[...] * lk1_ref[...])) - jnp.exp(jnp.sum(lq2_ref[...] * lk2_ref[...]))
           + LAMBDA_INIT)
    comp1_lane = lax.broadcasted_iota(jnp.int32, (tq, HEAD_DIM), 1) < DIFF_HALF

    def load_q_parts(t):
        q = q_ref[pl.ds(pl.multiple_of(t * tq, tq), tq), :]
        zero = jnp.zeros_like(q)
        return jnp.where(comp1_lane, q, zero), jnp.where(comp1_lane, zero, q)

    def score_chunk(slot, c, q_parts, m8):
        ks = k_ref[c * ATTN_KC:(c + 1) * ATTN_KC, :]
        new_m8 = []
        for comp in range(2):
            s = lax.dot_general(ks, q_parts[comp], (((1,), (1,)), ((), ())), preferred_element_type=F32)
            s_ref[slot, comp, c * ATTN_KC:(c + 1) * ATTN_KC, :] = s
            cm = jnp.max(s.reshape(fold, SUBLANES, tq), axis=0)
            new_m8.append(cm if m8 is None else jnp.maximum(m8[comp], cm))
        return new_m8

    def prob_chunk(slot, c, m, acc):
        vts = vt_ref[:, c * ATTN_KC:(c + 1) * ATTN_KC]
        new_acc = []
        for comp in range(2):
            l8, o = acc[comp]
            p = jnp.exp2(s_ref[slot, comp, c * ATTN_KC:(c + 1) * ATTN_KC, :] - m[comp])
            l8 = l8 + jnp.sum(p.reshape(fold, SUBLANES, tq), axis=0)
            o = o + jnp.dot(vts, p.astype(BF16), preferred_element_type=F32)
            new_acc.append((l8, o))
        return new_acc

    def col_max(m8):
        return [jnp.max(x, axis=0, keepdims=True) for x in m8]

    def zero_acc():
        return [(jnp.zeros((SUBLANES, tq), F32), jnp.zeros((HEAD_DIM, tq), F32)) for _ in range(2)]

    def finish(t, acc):
        outs = [o / jnp.sum(l8, axis=0, keepdims=True) for l8, o in acc]
        out = (outs[0] - lam * outs[1]).T
        y = out * _rms_scale(out) * g_ref[...]
        o_ref[pl.ds(pl.multiple_of(t * tq, tq), tq), :] = (y * (1.0 - LAMBDA_INIT)).astype(o_ref.dtype)

    def stage(t_next, t_cur, m_cur, t_prev, acc_prev):
        q_parts = load_q_parts(t_next[0]) if t_next is not None else None
        m8, acc = None, zero_acc()
        for c in range(n_chunks):
            if t_next is not None:
                m8 = score_chunk(t_next[1], c, q_parts, m8)
            if t_cur is not None:
                acc = prob_chunk(t_cur[1], c, m_cur, acc)
            if c == 0 and t_prev is not None:
                finish(t_prev, acc_prev)
        return (col_max(m8) if m8 is not None else None), acc

    m_cur, _ = stage((0, 0), None, None, None, None)
    m_cur, acc = stage((1, 1), (0, 0), m_cur, None, None)

    def pair(u, carry):
        m_odd, acc_even = carry
        m_even, acc_odd = stage((2 * u, 0), (2 * u - 1, 1), m_odd, 2 * u - 2, acc_even)
        m_odd, acc_even = stage((2 * u + 1, 1), (2 * u, 0), m_even, 2 * u - 1, acc_odd)
        return m_odd, acc_even

    m_cur, acc = lax.fori_loop(1, n_tiles // 2, pair, (m_cur, acc))
    _, acc_last = stage(None, (n_tiles - 1, 1), m_cur, n_tiles - 2, acc)
    finish(n_tiles - 1, acc_last)


def _diff_attn(proj, lq1, lk1, lq2, lk2, subln_g, batch, seq):
    lam_spec = pl.BlockSpec((1, DIFF_HALF), lambda b, h: (0, 0))
    head_blk = lambda col0: pl.BlockSpec((seq, HEAD_DIM), lambda b, h: (b, col0 + h))
    return pl.pallas_call(
        _diff_attn_kernel,
        grid=(batch, N_HEADS),
        in_specs=[
            lam_spec, lam_spec, lam_spec, lam_spec,
            pl.BlockSpec((1, HEAD_DIM), lambda b, h: (0, 0)),
            head_blk(GROUP_DQ * N_HEADS), head_blk(GROUP_DK * N_HEADS), head_blk(GROUP_DV * N_HEADS),
        ],
        out_specs=pl.BlockSpec((seq, HEAD_DIM), lambda b, h: (b, h)),
        out_shape=jax.ShapeDtypeStruct((batch * seq, GROUP_WIDTH), BF16),
        scratch_shapes=[pltpu.VMEM((HEAD_DIM, seq), BF16), pltpu.VMEM((2, 2, seq, ATTN_TQ), F32)],
        compiler_params=_compiler_params(("arbitrary", "arbitrary")),
        name="diff_attn",
    )(lq1, lk1, lq2, lk2, subln_g, proj, proj, proj)


N_REL_ROWS = 2 * WIN_H - 1
N_REL_COLS = 2 * WIN_W - 1
N_PAIR_TABLES = N_REL_ROWS - 1


def _na_kernel(rb_ref, q_ref, k_ref, v_ref, o_ref, tab_ref):
    head = pl.program_id(0)
    rows = q_ref.shape[0] // GRID_W
    band = WIN_H * GRID_W

    @pl.when(pl.program_id(1) == 0)
    def _():
        qc = lax.broadcasted_iota(jnp.int32, (GRID_W, LANES), 0)
        ln = lax.broadcasted_iota(jnp.int32, (GRID_W, LANES), 1)
        kc = ln & (GRID_W - 1)
        upper = ln >= GRID_W
        rel = jnp.clip(kc - qc + (WIN_W - 1), 0, N_REL_COLS - 1)
        col_start = jnp.clip(qc - WIN_W // 2, 0, GRID_W - WIN_W)
        inside = (kc >= col_start) & (kc < col_start + WIN_W)
        base = head * (N_REL_ROWS * N_REL_COLS)

        def build(t, carry):
            acc = jnp.zeros((GRID_W, LANES), F32)
            for d in range(N_REL_COLS):
                lo = rb_ref[base + t * N_REL_COLS + d]
                hi = rb_ref[base + (t + 1) * N_REL_COLS + d]
                acc = jnp.where(rel == d, jnp.where(upper, hi, lo), acc)
            tab_ref[t] = jnp.where(inside, acc * LOG2_E, NEG_INF)
            return carry

        lax.fori_loop(0, N_PAIR_TABLES, build, 0)

    def band_start(r):
        return min(max(r - WIN_H // 2, 0), rows - WIN_H)

    def score(r):
        q = q_ref[r * GRID_W:(r + 1) * GRID_W, :]
        kb = k_ref[band_start(r) * GRID_W:band_start(r) * GRID_W + band, :]
        return lax.dot_general(q, kb, (((1,), (1,)), ((), ())), preferred_element_type=F32)

    def softmax(r, s):
        rel0 = band_start(r) - r + (WIN_H - 1)
        parts = [s[:, jj * LANES:(jj + 1) * LANES] + tab_ref[rel0 + 2 * jj] for jj in range(band // LANES)]
        logits = jnp.concatenate(parts, axis=-1)
        p = jnp.exp2(logits - jnp.max(logits, axis=-1, keepdims=True))
        return p.astype(BF16), jnp.sum(p, axis=-1, keepdims=True)

    def value(r, p, l):
        vb = v_ref[band_start(r) * GRID_W:band_start(r) * GRID_W + band, :]
        o_ref[r * GRID_W:(r + 1) * GRID_W, :] = (jnp.dot(p, vb, preferred_element_type=F32) / l).astype(o_ref.dtype)

    groups = [range(g, g + NA_ROWS_PER_GROUP) for g in range(0, rows, NA_ROWS_PER_GROUP)]
    scores = {r: score(r) for r in groups[0]}
    for gi, group in enumerate(groups):
        probs = {r: softmax(r, scores.pop(r)) for r in group}
        nxt = groups[gi + 1] if gi + 1 < len(groups) else ()
        for u, r in enumerate(group):
            if u < len(nxt):
                scores[nxt[u]] = score(nxt[u])
            value(r, *probs[r])


def _na_attn(proj, rel_bias_flat, batch, seq):
    blk = lambda col0: pl.BlockSpec((seq, HEAD_DIM), lambda h, b: (b, col0 + h))
    return pl.pallas_call(
        _na_kernel,
        grid=(N_HEADS, batch),
        in_specs=[
            pl.BlockSpec(memory_space=pltpu.SMEM),
            blk(GROUP_NQ * N_HEADS), blk(GROUP_NK * N_HEADS), blk(GROUP_NV * N_HEADS),
        ],
        out_specs=pl.BlockSpec((seq, HEAD_DIM), lambda h, b: (b, h)),
        out_shape=jax.ShapeDtypeStruct((batch * seq, GROUP_WIDTH), BF16),
        scratch_shapes=[pltpu.VMEM((N_PAIR_TABLES, GRID_W, LANES), F32)],
        compiler_params=_compiler_params(("arbitrary", "arbitrary")),
        name="na_attn",
    )(rel_bias_flat, proj, proj, proj)


def _out_proj_kernel(x_ref, a_ref, n_ref, wa_ref, wn_ref, g_ref, x1_ref, h_ref):
    for mc in range(x_ref.shape[0] // OUT_MC):
        rows = slice(mc * OUT_MC, (mc + 1) * OUT_MC)
        acc = jnp.dot(a_ref[rows, :], wa_ref[...], preferred_element_type=F32)
        acc = acc + jnp.dot(n_ref[rows, :], wn_ref[...], preferred_element_type=F32)
        x1 = x_ref[rows, :] + acc
        x1_ref[rows, :] = x1
        h_ref[rows, :] = (x1 * _rms_scale(x1) * g_ref[...]).astype(BF16)


def _out_proj(x2, a_out, n_out, w_out_bf16, g_mlp):
    rows = x2.shape[0]
    tm = OUT_TM
    row_blk = lambda width: pl.BlockSpec((tm, width), lambda i: (i, 0))
    return pl.pallas_call(
        _out_proj_kernel,
        grid=(rows // tm,),
        in_specs=[
            row_blk(D_MODEL), row_blk(GROUP_WIDTH), row_blk(GROUP_WIDTH),
            pl.BlockSpec((GROUP_WIDTH, D_MODEL), lambda i: (0, 0)),
            pl.BlockSpec((GROUP_WIDTH, D_MODEL), lambda i: (1, 0)),
            pl.BlockSpec((1, D_MODEL), lambda i: (0, 0)),
        ],
        out_specs=[row_blk(D_MODEL), row_blk(D_MODEL)],
        out_shape=[jax.ShapeDtypeStruct((rows, D_MODEL), F32), jax.ShapeDtypeStruct((rows, D_MODEL), BF16)],
        compiler_params=_compiler_params(("arbitrary",)),
        name="out_proj",
    )(x2, a_out, n_out, w_out_bf16, w_out_bf16, g_mlp)


def _mlp_kernel(h_ref, x1_ref, wu_ref, wd_ref, gf_ref, o_ref):
    j = pl.program_id(1)
    last_j = pl.num_programs(1) - 1
    tm = o_ref.shape[0]
    piece = x1_ref.shape[0]
    chunks = [(start, start + MLP_MC) for start in range(0, tm, MLP_MC)]
    n_chunks = len(chunks)

    def step(first, last):
        wu = wu_ref[...].astype(BF16)
        wd = wd_ref[...].astype(BF16)

        def up(c):
            u = jnp.dot(h_ref[slice(*chunks[c]), :], wu, preferred_element_type=F32)
            r = jnp.maximum(u, 0.0)
            return (r * r).astype(BF16)

        def down(c, act):
            rows = slice(*chunks[c])
            y = jnp.dot(act, wd, preferred_element_type=F32)
            if not first:
                y = y + o_ref[rows, :]
            if last:
                y = y * _rms_scale(y) * gf_ref[...]
            o_ref[rows, :] = y

        if last:
            o_ref[tm - piece:tm, :] += x1_ref[...]
        act = up(0)
        for c in range(n_chunks):
            nxt = up(c + 1) if c + 1 < n_chunks else None
            down(c, act)
            act = nxt
        if first:
            o_ref[0:piece, :] += x1_ref[...]
        elif not last:
            o_ref[pl.ds(pl.multiple_of(j * piece, piece), piece), :] += x1_ref[...]

    pl.when(j == 0)(functools.partial(step, True, False))
    pl.when((j > 0) & (j < last_j))(functools.partial(step, False, False))
    pl.when(j == last_j)(functools.partial(step, False, True))


def _mlp(h, x1, w_up_f32, w_down_f32, g_final):
    rows = x1.shape[0]
    tm, tf = MLP_TM, MLP_TF
    steps = D_FF // tf
    piece = tm // steps
    return pl.pallas_call(
        _mlp_kernel,
        grid=(rows // tm, steps),
        in_specs=[
            pl.BlockSpec((tm, D_MODEL), lambda i, j: (i, 0)),
            pl.BlockSpec((piece, D_MODEL), lambda i, j: (i * steps + j, 0)),
            pl.BlockSpec((D_MODEL, tf), lambda i, j: (0, j)),
            pl.BlockSpec((tf, D_MODEL), lambda i, j: (j, 0)),
            pl.BlockSpec((1, D_MODEL), lambda i, j: (0, 0)),
        ],
        out_specs=pl.BlockSpec((tm, D_MODEL), lambda i, j: (i, 0)),
        out_shape=jax.ShapeDtypeStruct((rows, D_MODEL), F32),
        compiler_params=_compiler_params(("arbitrary", "arbitrary")),
        name="mlp",
    )(h, x1, w_up_f32, w_down_f32, g_final)


def kernel(x, norm_mix_g, w_in, lambda_q1, lambda_k1, lambda_q2, lambda_k2, diff_subln_g, na_rel_bias, w_out,
           norm_mlp_g, w_up, w_down, norm_final_g):
    batch, seq, d_model = x.shape
    assert d_model == D_MODEL and w_in.shape == (1, D_MODEL, N_GROUPS * GROUP_WIDTH)
    assert seq % PROJ_TM == 0 and seq % ATTN_TQ == 0 and seq % ATTN_KC == 0 and seq % (GRID_W * WIN_H) == 0
    x2 = x.reshape(batch * seq, D_MODEL)

    proj = _in_proj(x2, norm_mix_g, w_in[0], seq)
    a_out = _diff_attn(proj, lambda_q1, lambda_k1, lambda_q2, lambda_k2, diff_subln_g, batch, seq)
    n_out = _na_attn(proj, na_rel_bias[0].reshape(-1), batch, seq)
    x1, h_mlp = _out_proj(x2, a_out, n_out, w_out[0].astype(BF16), norm_mlp_g)
    y = _mlp(h_mlp, x1, w_up[0], w_down[0], norm_final_g.reshape(1, D_MODEL))
    return y.reshape(batch, seq, D_MODEL)
```

```python
import functools
import math

import jax
import jax.numpy as jnp
import numpy as np
from jax import lax
from jax.experimental import pallas as pl
from jax.experimental.pallas import tpu as pltpu

F32 = jnp.float32
BF16 = jnp.bfloat16

D_MODEL = 2048
HEAD_DIM = 128
N_HEADS = 8
GROUP_WIDTH = N_HEADS * HEAD_DIM
N_GROUPS = 6
GROUP_DQ, GROUP_DK, GROUP_DV, GROUP_NQ, GROUP_NK, GROUP_NV = range(N_GROUPS)
DIFF_HALF = HEAD_DIM // 2
ROT_DIM = DIFF_HALF // 4
ROT_HALF = ROT_DIM // 2
ROPE_THETA = 500000.0
GRID_W = 64
WIN_H = 8
WIN_W = 16
D_FF = 4 * D_MODEL
EPS = 1e-5
NEG_INF = -1e30
LAMBDA_INIT = 0.8 - 0.6 * math.exp(-0.3 * 0)

LANES = 128
SUBLANES = 8
VMEM_LIMIT_BYTES = 56 * 1024 * 1024

PROJ_TM = 512
PROJ_MC = 256
PROJ_GROUPS_PER_STEP = 2
PROJ_W_CHUNK = 256
ATTN_TQ = 256
ATTN_KC = 512
OUT_TM = 512
OUT_MC = 256
MLP_TM = 1024
MLP_TF = 512
MLP_MC = 512
NA_ROWS_PER_GROUP = 16


def _compiler_params(semantics):
    return pltpu.CompilerParams(dimension_semantics=semantics, vmem_limit_bytes=VMEM_LIMIT_BYTES)


def _row_chunks(total, chunk):
    bounds = list(range(0, total, chunk)) + [total]
    bounds.insert(-1, bounds[-2] + chunk // 2)
    return list(zip(bounds[:-1], bounds[1:]))


def _rms_scale(x):
    return lax.rsqrt(jnp.mean(x * x, axis=-1, keepdims=True) + EPS)


LOG2_E = math.log2(math.e)
SCORE_SCALE_LOG2 = LOG2_E / math.sqrt(DIFF_HALF)
NA_SCALE_LOG2 = LOG2_E / math.sqrt(HEAD_DIM)


def _in_proj_kernel(x_ref, g_ref, w_hbm_ref, c_ref, s_lo_ref, s_hi_ref, o_ref, h_ref, w_ref, stage_ref, sem):
    i = pl.program_id(0)
    j = pl.program_id(1)
    step_cols = PROJ_GROUPS_PER_STEP * GROUP_WIDTH
    n_pieces = step_cols // PROJ_W_CHUNK

    def piece_copy(p):
        col0 = pl.multiple_of(j * step_cols + p * PROJ_W_CHUNK, PROJ_W_CHUNK)
        slot = p % 2
        return pltpu.make_async_copy(w_hbm_ref.at[:, pl.ds(col0, PROJ_W_CHUNK)], stage_ref.at[slot], sem.at[slot])

    @pl.when(i == 0)
    def _():
        piece_copy(0).start()
        for p in range(n_pieces):
            if p + 1 < n_pieces:
                piece_copy(p + 1).start()
            piece_copy(p).wait()
            w_ref[j, :, p * PROJ_W_CHUNK:(p + 1) * PROJ_W_CHUNK] = stage_ref[p % 2].astype(BF16)

    def step(groups):
        first = groups[0] == GROUP_DQ
        w = w_ref[j]
        for start, stop in _row_chunks(x_ref.shape[0], PROJ_MC):
            rows = slice(start, stop)
            if first:
                x = x_ref[rows, :]
                h_ref[rows, :] = (x * _rms_scale(x) * g_ref[...]).astype(BF16)
            acc = jnp.dot(h_ref[rows, :], w, preferred_element_type=F32)
            for gi, group in enumerate(groups):
                scale = {GROUP_DQ: SCORE_SCALE_LOG2, GROUP_NQ: NA_SCALE_LOG2}.get(group)
                scaled = (lambda t: t * scale) if scale is not None else (lambda t: t)
                cols = slice(gi * GROUP_WIDTH, (gi + 1) * GROUP_WIDTH)
                if group not in (GROUP_DQ, GROUP_DK):
                    o_ref[rows, cols] = scaled(acc[:, cols]).astype(BF16)
                    continue
                c = scaled(c_ref[rows, :])
                s_lo = scaled(s_lo_ref[rows, :])
                s_hi = scaled(s_hi_ref[rows, :])
                for hh in range(N_HEADS):
                    lanes = slice(cols.start + hh * HEAD_DIM, cols.start + (hh + 1) * HEAD_DIM)
                    a = acc[:, lanes]
                    r = a * c + pltpu.roll(a, LANES - ROT_HALF, 1) * s_lo + pltpu.roll(a, ROT_HALF, 1) * s_hi
                    o_ref[rows, lanes] = r.astype(BF16)

    for col_step in range(N_GROUPS // PROJ_GROUPS_PER_STEP):
        groups = tuple(range(col_step * PROJ_GROUPS_PER_STEP, (col_step + 1) * PROJ_GROUPS_PER_STEP))
        pl.when(j == col_step)(functools.partial(step, groups))


@functools.lru_cache(maxsize=None)
def _rotary_lane_tables(seq):
    lane = np.arange(LANES)
    inv_freq = np.power(np.float32(ROPE_THETA), -np.arange(0, ROT_DIM, 2, dtype=np.float32) / np.float32(ROT_DIM))
    ang = (np.arange(seq, dtype=np.float32)[:, None] * inv_freq[lane % ROT_HALF][None, :]).astype(np.float32)
    in_lo = (lane % DIFF_HALF) < ROT_HALF
    in_hi = ((lane % DIFF_HALF) >= ROT_HALF) & ((lane % DIFF_HALF) < ROT_DIM)
    cos, sin = np.cos(ang).astype(np.float32), np.sin(ang).astype(np.float32)
    c = np.where(in_lo | in_hi, cos, np.float32(1.0))
    s_lo = np.where(in_lo, -sin, np.float32(0.0))
    s_hi = np.where(in_hi, sin, np.float32(0.0))
    return c, s_lo, s_hi


def _in_proj(x2, g, w_f32, seq):
    rows = x2.shape[0]
    tm = PROJ_TM
    col_steps = N_GROUPS // PROJ_GROUPS_PER_STEP
    step_cols = PROJ_GROUPS_PER_STEP * GROUP_WIDTH
    tables = [jnp.asarray(t) for t in _rotary_lane_tables(seq)]
    pos_blocks = seq // tm
    tab_spec = pl.BlockSpec((tm, LANES), lambda i, j: (i % pos_blocks, 0))
    return pl.pallas_call(
        _in_proj_kernel,
        grid=(rows // tm, col_steps),
        in_specs=[
            pl.BlockSpec((tm, D_MODEL), lambda i, j: (i, 0)),
            pl.BlockSpec((1, D_MODEL), lambda i, j: (0, 0)),
            pl.BlockSpec(memory_space=pl.ANY),
            tab_spec, tab_spec, tab_spec,
        ],
        out_specs=pl.BlockSpec((tm, step_cols), lambda i, j: (i, j)),
        out_shape=jax.ShapeDtypeStruct((rows, N_GROUPS * GROUP_WIDTH), BF16),
        scratch_shapes=[
            pltpu.VMEM((tm, D_MODEL), BF16),
            pltpu.VMEM((col_steps, D_MODEL, step_cols), BF16),
            pltpu.VMEM((2, D_MODEL, PROJ_W_CHUNK), F32),
            pltpu.SemaphoreType.DMA((2,)),
        ],
        compiler_params=_compiler_params(("arbitrary", "arbitrary")),
        name="in_proj",
    )(x2, g, w_f32, *tables)


def _diff_attn_kernel(lq1_ref, lk1_ref, lq2_ref, lk2_ref, g_ref, q_ref, k_ref, v_ref, o_ref, vt_ref, s_ref):
    seq = k_ref.shape[0]
    tq = ATTN_TQ
    n_tiles = seq // tq
    n_chunks = seq // ATTN_KC
    fold = ATTN_KC // SUBLANES
    assert n_tiles % 2 == 0 and n_tiles >= 4

    vt_ref[...] = v_ref[...].T
    lam = (jnp.exp(jnp.sum(lq1_ref[...] * lk1_ref[...])) - jnp.exp(jnp.sum(lq2_ref[...] * lk2_ref[...]))
           + LAMBDA_INIT)
    comp1_lane = lax.broadcasted_iota(jnp.int32, (tq, HEAD_DIM), 1) < DIFF_HALF

    def load_q_parts(t):
        q = q_ref[pl.ds(pl.multiple_of(t * tq, tq), tq), :]
        zero = jnp.zeros_like(q)
        return jnp.where(comp1_lane, q, zero), jnp.where(comp1_lane, zero, q)

    def score_chunk(slot, c, q_parts, m8):
        ks = k_ref[c * ATTN_KC:(c + 1) * ATTN_KC, :]
        new_m8 = []
        for comp in range(2):
            s = lax.dot_general(ks, q_parts[comp], (((1,), (1,)), ((), ())), preferred_element_type=F32)
            s_ref[slot, comp, c * ATTN_KC:(c + 1) * ATTN_KC, :] = s
            cm = jnp.max(s.reshape(fold, SUBLANES, tq), axis=0)
            new_m8.append(cm if m8 is None else jnp.maximum(m8[comp], cm))
        return new_m8

    def prob_chunk(slot, c, m, acc):
        vts = vt_ref[:, c * ATTN_KC:(c + 1) * ATTN_KC]
        new_acc = []
        for comp in range(2):
            l8, o = acc[comp]
            p = jnp.exp2(s_ref[slot, comp, c * ATTN_KC:(c + 1) * ATTN_KC, :] - m[comp])
            l8 = l8 + jnp.sum(p.reshape(fold, SUBLANES, tq), axis=0)
            o = o + jnp.dot(vts, p.astype(BF16), preferred_element_type=F32)
            new_acc.append((l8, o))
        return new_acc

    def col_max(m8):
        return [jnp.max(x, axis=0, keepdims=True) for x in m8]

    def zero_acc():
        return [(jnp.zeros((SUBLANES, tq), F32), jnp.zeros((HEAD_DIM, tq), F32)) for _ in range(2)]

    def finish(t, acc):
        outs = [o / jnp.sum(l8, axis=0, keepdims=True) for l8, o in acc]
        out = (outs[0] - lam * outs[1]).T
        y = out * _rms_scale(out) * g_ref[...]
        o_ref[pl.ds(pl.multiple_of(t * tq, tq), tq), :] = (y * (1.0 - LAMBDA_INIT)).astype(o_ref.dtype)

    def stage(t_next, t_cur, m_cur, t_prev, acc_prev):
        q_parts = load_q_parts(t_next[0]) if t_next is not None else None
        m8, acc = None, zero_acc()
        for c in range(n_chunks):
            if t_next is not None:
                m8 = score_chunk(t_next[1], c, q_parts, m8)
            if t_cur is not None:
                acc = prob_chunk(t_cur[1], c, m_cur, acc)
            if c == 0 and t_prev is not None:
                finish(t_prev, acc_prev)
        return (col_max(m8) if m8 is not None else None), acc

    m_cur, _ = stage((0, 0), None, None, None, None)
    m_cur, acc = stage((1, 1), (0, 0), m_cur, None, None)

    def pair(u, carry):
        m_odd, acc_even = carry
        m_even, acc_odd = stage((2 * u, 0), (2 * u - 1, 1), m_odd, 2 * u - 2, acc_even)
        m_odd, acc_even = stage((2 * u + 1, 1), (2 * u, 0), m_even, 2 * u - 1, acc_odd)
        return m_odd, acc_even

    m_cur, acc = lax.fori_loop(1, n_tiles // 2, pair, (m_cur, acc))
    _, acc_last = stage(None, (n_tiles - 1, 1), m_cur, n_tiles - 2, acc)
    finish(n_tiles - 1, acc_last)


def _diff_attn(proj, lq1, lk1, lq2, lk2, subln_g, batch, seq):
    lam_spec = pl.BlockSpec((1, DIFF_HALF), lambda b, h: (0, 0))
    head_blk = lambda col0: pl.BlockSpec((seq, HEAD_DIM), lambda b, h: (b, col0 + h))
    return pl.pallas_call(
        _diff_attn_kernel,
        grid=(batch, N_HEADS),
        in_specs=[
            lam_spec, lam_spec, lam_spec, lam_spec,
            pl.BlockSpec((1, HEAD_DIM), lambda b, h: (0, 0)),
            head_blk(GROUP_DQ * N_HEADS), head_blk(GROUP_DK * N_HEADS), head_blk(GROUP_DV * N_HEADS),
        ],
        out_specs=pl.BlockSpec((seq, HEAD_DIM), lambda b, h: (b, h)),
        out_shape=jax.ShapeDtypeStruct((batch * seq, GROUP_WIDTH), BF16),
        scratch_shapes=[pltpu.VMEM((HEAD_DIM, seq), BF16), pltpu.VMEM((2, 2, seq, ATTN_TQ), F32)],
        compiler_params=_compiler_params(("arbitrary", "arbitrary")),
        name="diff_attn",
    )(lq1, lk1, lq2, lk2, subln_g, proj, proj, proj)


N_REL_ROWS = 2 * WIN_H - 1
N_REL_COLS = 2 * WIN_W - 1
N_PAIR_TABLES = N_REL_ROWS - 1


def _na_kernel(rb_ref, q_ref, k_ref, v_ref, o_ref, tab_ref, *, batch):
    head = pl.program_id(0)
    rows = q_ref.shape[0] // (batch * GRID_W)
    band = WIN_H * GRID_W

    qc = lax.broadcasted_iota(jnp.int32, (GRID_W, LANES), 0)
    ln = lax.broadcasted_iota(jnp.int32, (GRID_W, LANES), 1)
    kc = ln & (GRID_W - 1)
    upper = ln >= GRID_W
    rel = jnp.clip(kc - qc + (WIN_W - 1), 0, N_REL_COLS - 1)
    col_start = jnp.clip(qc - WIN_W // 2, 0, GRID_W - WIN_W)
    inside = (kc >= col_start) & (kc < col_start + WIN_W)
    base = head * (N_REL_ROWS * N_REL_COLS)

    def build(t, carry):
        acc = jnp.zeros((GRID_W, LANES), F32)
        for d in range(N_REL_COLS):
            lo = rb_ref[base + t * N_REL_COLS + d]
            hi = rb_ref[base + (t + 1) * N_REL_COLS + d]
            acc = jnp.where(rel == d, jnp.where(upper, hi, lo), acc)
        tab_ref[t] = jnp.where(inside, acc * LOG2_E, NEG_INF)
        return carry

    lax.fori_loop(0, N_PAIR_TABLES, build, 0)

    def band_start(r):
        return min(max(r - WIN_H // 2, 0), rows - WIN_H)

    def token_rows(b, r, n_rows):
        start = (b * rows + r) * GRID_W
        return slice(start, start + n_rows * GRID_W)

    def score(b, r):
        q = q_ref[token_rows(b, r, 1), :]
        kb = k_ref[token_rows(b, band_start(r), WIN_H), :]
        return lax.dot_general(q, kb, (((1,), (1,)), ((), ())), preferred_element_type=F32)

    def softmax(r, s):
        rel0 = band_start(r) - r + (WIN_H - 1)
        parts = [s[:, jj * LANES:(jj + 1) * LANES] + tab_ref[rel0 + 2 * jj] for jj in range(band // LANES)]
        logits = jnp.concatenate(parts, axis=-1)
        p = jnp.exp2(logits - jnp.max(logits, axis=-1, keepdims=True))
        return p.astype(BF16), jnp.sum(p, axis=-1, keepdims=True)

    def value(b, r, p, l):
        vb = v_ref[token_rows(b, band_start(r), WIN_H), :]
        o_ref[token_rows(b, r, 1), :] = (jnp.dot(p, vb, preferred_element_type=F32) / l).astype(o_ref.dtype)

    units = [(b, r) for b in range(batch) for r in range(rows)]
    groups = [units[g:g + NA_ROWS_PER_GROUP] for g in range(0, len(units), NA_ROWS_PER_GROUP)]
    scores = {u: score(*u) for u in groups[0]}
    for gi, group in enumerate(groups):
        probs = {u: softmax(u[1], scores.pop(u)) for u in group}
        nxt = groups[gi + 1] if gi + 1 < len(groups) else ()
        for i, u in enumerate(group):
            if i < len(nxt):
                scores[nxt[i]] = score(*nxt[i])
            value(*u, *probs[u])


def _na_attn(proj, rel_bias_flat, batch, seq):
    blk = lambda group: pl.BlockSpec((batch * seq, HEAD_DIM), lambda h: (0, group * N_HEADS + h))
    return pl.pallas_call(
        functools.partial(_na_kernel, batch=batch),
        grid=(N_HEADS,),
        in_specs=[
            pl.BlockSpec(memory_space=pltpu.SMEM),
            blk(GROUP_NQ), blk(GROUP_NK), blk(GROUP_NV),
        ],
        out_specs=pl.BlockSpec((batch * seq, HEAD_DIM), lambda h: (0, h)),
        out_shape=jax.ShapeDtypeStruct((batch * seq, GROUP_WIDTH), BF16),
        scratch_shapes=[pltpu.VMEM((N_PAIR_TABLES, GRID_W, LANES), F32)],
        compiler_params=_compiler_params(("arbitrary",)),
        name="na_attn",
    )(rel_bias_flat, proj, proj, proj)


def _out_proj_kernel(x_ref, a_ref, n_ref, wa_ref, wn_ref, g_ref, x1_ref, h_ref, wa_bf_ref, wn_bf_ref):
    @pl.when(pl.program_id(0) == 0)
    def _():
        wa_bf_ref[...] = wa_ref[...].astype(BF16)
        wn_bf_ref[...] = wn_ref[...].astype(BF16)

    for mc in range(x_ref.shape[0] // OUT_MC):
        rows = slice(mc * OUT_MC, (mc + 1) * OUT_MC)
        acc = jnp.dot(a_ref[rows, :], wa_bf_ref[...], preferred_element_type=F32)
        acc = acc + jnp.dot(n_ref[rows, :], wn_bf_ref[...], preferred_element_type=F32)
        x1 = x_ref[rows, :] + acc
        x1_ref[rows, :] = x1
        h_ref[rows, :] = (x1 * _rms_scale(x1) * g_ref[...]).astype(BF16)


def _out_proj(x2, a_out, n_out, w_out_f32, g_mlp):
    rows = x2.shape[0]
    tm = OUT_TM
    row_blk = lambda width: pl.BlockSpec((tm, width), lambda i: (i, 0))
    w_half = lambda half: pl.BlockSpec((GROUP_WIDTH, D_MODEL), lambda i: (half, 0), pipeline_mode=pl.Buffered(1))
    return pl.pallas_call(
        _out_proj_kernel,
        grid=(rows // tm,),
        in_specs=[
            row_blk(D_MODEL), row_blk(GROUP_WIDTH), row_blk(GROUP_WIDTH),
            w_half(0), w_half(1),
            pl.BlockSpec((1, D_MODEL), lambda i: (0, 0)),
        ],
        out_specs=[row_blk(D_MODEL), row_blk(D_MODEL)],
        out_shape=[jax.ShapeDtypeStruct((rows, D_MODEL), F32), jax.ShapeDtypeStruct((rows, D_MODEL), BF16)],
        scratch_shapes=[pltpu.VMEM((GROUP_WIDTH, D_MODEL), BF16), pltpu.VMEM((GROUP_WIDTH, D_MODEL), BF16)],
        compiler_params=_compiler_params(("arbitrary",)),
        name="out_proj",
    )(x2, a_out, n_out, w_out_f32, w_out_f32, g_mlp)


def _mlp_kernel(h_ref, x1_ref, wu_ref, wd_ref, gf_ref, o_ref):
    j = pl.program_id(1)
    last_j = pl.num_programs(1) - 1
    tm = o_ref.shape[0]
    piece = x1_ref.shape[0]
    chunks = [(start, start + MLP_MC) for start in range(0, tm, MLP_MC)]
    n_chunks = len(chunks)

    def step(first, last):
        wu = wu_ref[...].astype(BF16)
        wd = wd_ref[...].astype(BF16)

        def up(c):
            u = jnp.dot(h_ref[slice(*chunks[c]), :], wu, preferred_element_type=F32)
            r = jnp.maximum(u, 0.0)
            return (r * r).astype(BF16)

        def down(c, act):
            rows = slice(*chunks[c])
            y = jnp.dot(act, wd, preferred_element_type=F32)
            if not first:
                y = y + o_ref[rows, :]
            if last:
                y = y * _rms_scale(y) * gf_ref[...]
            o_ref[rows, :] = y

        if last:
            o_ref[tm - piece:tm, :] += x1_ref[...]
        act = up(0)
        for c in range(n_chunks):
            nxt = up(c + 1) if c + 1 < n_chunks else None
            down(c, act)
            act = nxt
        if first:
            o_ref[0:piece, :] += x1_ref[...]
        elif not last:
            o_ref[pl.ds(pl.multiple_of(j * piece, piece), piece), :] += x1_ref[...]

    pl.when(j == 0)(functools.partial(step, True, False))
    pl.when((j > 0) & (j < last_j))(functools.partial(step, False, False))
    pl.when(j == last_j)(functools.partial(step, False, True))


def _mlp(h, x1, w_up_f32, w_down_f32, g_final):
    rows = x1.shape[0]
    tm, tf = MLP_TM, MLP_TF
    steps = D_FF // tf
    piece = tm // steps
    return pl.pallas_call(
        _mlp_kernel,
        grid=(rows // tm, steps),
        in_specs=[
            pl.BlockSpec((tm, D_MODEL), lambda i, j: (i, 0)),
            pl.BlockSpec((piece, D_MODEL), lambda i, j: (i * steps + j, 0)),
            pl.BlockSpec((D_MODEL, tf), lambda i, j: (0, j)),
            pl.BlockSpec((tf, D_MODEL), lambda i, j: (j, 0)),
            pl.BlockSpec((1, D_MODEL), lambda i, j: (0, 0)),
        ],
        out_specs=pl.BlockSpec((tm, D_MODEL), lambda i, j: (i, 0)),
        out_shape=jax.ShapeDtypeStruct((rows, D_MODEL), F32),
        compiler_params=_compiler_params(("arbitrary", "arbitrary")),
        name="mlp",
    )(h, x1, w_up_f32, w_down_f32, g_final)


def kernel(x, norm_mix_g, w_in, lambda_q1, lambda_k1, lambda_q2, lambda_k2, diff_subln_g, na_rel_bias, w_out,
           norm_mlp_g, w_up, w_down, norm_final_g):
    batch, seq, d_model = x.shape
    assert d_model == D_MODEL and w_in.shape == (1, D_MODEL, N_GROUPS * GROUP_WIDTH)
    assert seq % PROJ_TM == 0 and seq % ATTN_TQ == 0 and seq % ATTN_KC == 0 and seq % (GRID_W * WIN_H) == 0
    x2 = x.reshape(batch * seq, D_MODEL)

    proj = _in_proj(x2, norm_mix_g, w_in[0], seq)
    a_out = _diff_attn(proj, lambda_q1, lambda_k1, lambda_q2, lambda_k2, diff_subln_g, batch, seq)
    n_out = _na_attn(proj, na_rel_bias[0].reshape(-1), batch, seq)
    x1, h_mlp = _out_proj(x2, a_out, n_out, w_out[0], norm_mlp_g)
    y = _mlp(h_mlp, x1, w_up[0], w_down[0], norm_final_g.reshape(1, D_MODEL))
    return y.reshape(batch, seq, D_MODEL)
```

```python
import functools
import math

import jax
import jax.numpy as jnp
import numpy as np
from jax import lax
from jax.experimental import pallas as pl
from jax.experimental.pallas import tpu as pltpu

F32 = jnp.float32
BF16 = jnp.bfloat16

D_MODEL = 2048
HEAD_DIM = 128
N_HEADS = 8
GROUP_WIDTH = N_HEADS * HEAD_DIM
N_GROUPS = 6
GROUP_DQ, GROUP_DK, GROUP_DV, GROUP_NQ, GROUP_NK, GROUP_NV = range(N_GROUPS)
DIFF_HALF = HEAD_DIM // 2
ROT_DIM = DIFF_HALF // 4
ROT_HALF = ROT_DIM // 2
ROPE_THETA = 500000.0
GRID_W = 64
WIN_H = 8
WIN_W = 16
D_FF = 4 * D_MODEL
EPS = 1e-5
NEG_INF = -1e30
LAMBDA_INIT = 0.8 - 0.6 * math.exp(-0.3 * 0)

LANES = 128
SUBLANES = 8
VMEM_LIMIT_BYTES = 60 * 1024 * 1024

PROJ_TM = 512
PROJ_MC = 256
PROJ_GROUPS_PER_STEP = 2
PROJ_W_CHUNK = 256
ATTN_TQ = 256
ATTN_KC = 512
OUT_TM = 512
OUT_MC = 256
MLP_TM = 1024
MLP_TF = 512
MLP_MC = 512
NA_ROWS_PER_GROUP = 16


def _compiler_params(semantics):
    return pltpu.CompilerParams(dimension_semantics=semantics, vmem_limit_bytes=VMEM_LIMIT_BYTES)


def _row_chunks(total, chunk):
    bounds = list(range(0, total, chunk)) + [total]
    bounds.insert(-1, bounds[-2] + chunk // 2)
    return list(zip(bounds[:-1], bounds[1:]))


def _rms_scale(x):
    return lax.rsqrt(jnp.mean(x * x, axis=-1, keepdims=True) + EPS)


LOG2_E = math.log2(math.e)
SCORE_SCALE_LOG2 = LOG2_E / math.sqrt(DIFF_HALF)
NA_SCALE_LOG2 = LOG2_E / math.sqrt(HEAD_DIM)


def _in_proj_kernel(x_ref, g_ref, w_hbm_ref, c_ref, s_lo_ref, s_hi_ref, o_ref, h_ref, w_ref, stage_ref, sem):
    i = pl.program_id(0)
    j = pl.program_id(1)
    n_col_steps = N_GROUPS // PROJ_GROUPS_PER_STEP
    step_cols = PROJ_GROUPS_PER_STEP * GROUP_WIDTH
    n_pieces = step_cols // PROJ_W_CHUNK
    row_chunks = _row_chunks(x_ref.shape[0], PROJ_MC)
    pieces_per_chunk = (n_pieces - 2) // len(row_chunks)
    assert pieces_per_chunk * len(row_chunks) == n_pieces - 2

    def piece_copy(col_step, p):
        col0 = col_step * step_cols + p * PROJ_W_CHUNK
        slot = p % 2
        return pltpu.make_async_copy(w_hbm_ref.at[:, pl.ds(col0, PROJ_W_CHUNK)], stage_ref.at[slot], sem.at[slot])

    def land_piece(col_step, p):
        piece_copy(col_step, p).wait()
        w_ref[col_step, :, p * PROJ_W_CHUNK:(p + 1) * PROJ_W_CHUNK] = stage_ref[p % 2].astype(BF16)
        if p + 2 < n_pieces:
            piece_copy(col_step, p + 2).start()

    def step(col_step, first_tile):
        groups = tuple(range(col_step * PROJ_GROUPS_PER_STEP, (col_step + 1) * PROJ_GROUPS_PER_STEP))
        first = groups[0] == GROUP_DQ
        prefetch = first_tile and col_step + 1 < n_col_steps
        if first_tile:
            if col_step == 0:
                piece_copy(0, 0).start()
                piece_copy(0, 1).start()
                for p in range(n_pieces):
                    land_piece(0, p)
            else:
                land_piece(col_step, n_pieces - 2)
                land_piece(col_step, n_pieces - 1)
            if prefetch:
                piece_copy(col_step + 1, 0).start()
                piece_copy(col_step + 1, 1).start()
        w = w_ref[col_step]
        for ci, (start, stop) in enumerate(row_chunks):
            rows = slice(start, stop)
            if first:
                x = x_ref[rows, :]
                h_ref[rows, :] = (x * _rms_scale(x) * g_ref[...]).astype(BF16)
            acc = jnp.dot(h_ref[rows, :], w, preferred_element_type=F32)
            if prefetch:
                for p in range(ci * pieces_per_chunk, (ci + 1) * pieces_per_chunk):
                    land_piece(col_step + 1, p)
            for gi, group in enumerate(groups):
                scale = {GROUP_DQ: SCORE_SCALE_LOG2, GROUP_NQ: NA_SCALE_LOG2}.get(group)
                scaled = (lambda t: t * scale) if scale is not None else (lambda t: t)
                cols = slice(gi * GROUP_WIDTH, (gi + 1) * GROUP_WIDTH)
                if group not in (GROUP_DQ, GROUP_DK):
                    o_ref[rows, cols] = scaled(acc[:, cols]).astype(BF16)
                    continue
                c = scaled(c_ref[rows, :])
                s_lo = scaled(s_lo_ref[rows, :])
                s_hi = scaled(s_hi_ref[rows, :])
                for hh in range(N_HEADS):
                    lanes = slice(cols.start + hh * HEAD_DIM, cols.start + (hh + 1) * HEAD_DIM)
                    a = acc[:, lanes]
                    r = a * c + pltpu.roll(a, LANES - ROT_HALF, 1) * s_lo + pltpu.roll(a, ROT_HALF, 1) * s_hi
                    o_ref[rows, lanes] = r.astype(BF16)

    for col_step in range(n_col_steps):
        pl.when((j == col_step) & (i == 0))(functools.partial(step, col_step, True))
        pl.when((j == col_step) & (i != 0))(functools.partial(step, col_step, False))


@functools.lru_cache(maxsize=None)
def _rotary_lane_tables(seq):
    lane = np.arange(LANES)
    inv_freq = np.power(np.float32(ROPE_THETA), -np.arange(0, ROT_DIM, 2, dtype=np.float32) / np.float32(ROT_DIM))
    ang = (np.arange(seq, dtype=np.float32)[:, None] * inv_freq[lane % ROT_HALF][None, :]).astype(np.float32)
    in_lo = (lane % DIFF_HALF) < ROT_HALF
    in_hi = ((lane % DIFF_HALF) >= ROT_HALF) & ((lane % DIFF_HALF) < ROT_DIM)
    cos, sin = np.cos(ang).astype(np.float32), np.sin(ang).astype(np.float32)
    c = np.where(in_lo | in_hi, cos, np.float32(1.0))
    s_lo = np.where(in_lo, -sin, np.float32(0.0))
    s_hi = np.where(in_hi, sin, np.float32(0.0))
    return c, s_lo, s_hi


def _in_proj(x2, g, w_f32, seq):
    rows = x2.shape[0]
    tm = PROJ_TM
    col_steps = N_GROUPS // PROJ_GROUPS_PER_STEP
    step_cols = PROJ_GROUPS_PER_STEP * GROUP_WIDTH
    tables = [jnp.asarray(t) for t in _rotary_lane_tables(seq)]
    pos_blocks = seq // tm
    tab_spec = pl.BlockSpec((tm, LANES), lambda i, j: (i % pos_blocks, 0))
    return pl.pallas_call(
        _in_proj_kernel,
        grid=(rows // tm, col_steps),
        in_specs=[
            pl.BlockSpec((tm, D_MODEL), lambda i, j: (i, 0)),
            pl.BlockSpec((1, D_MODEL), lambda i, j: (0, 0)),
            pl.BlockSpec(memory_space=pl.ANY),
            tab_spec, tab_spec, tab_spec,
        ],
        out_specs=pl.BlockSpec((tm, step_cols), lambda i, j: (i, j)),
        out_shape=jax.ShapeDtypeStruct((rows, N_GROUPS * GROUP_WIDTH), BF16),
        scratch_shapes=[
            pltpu.VMEM((tm, D_MODEL), BF16),
            pltpu.VMEM((col_steps, D_MODEL, step_cols), BF16),
            pltpu.VMEM((2, D_MODEL, PROJ_W_CHUNK), F32),
            pltpu.SemaphoreType.DMA((2,)),
        ],
        compiler_params=_compiler_params(("arbitrary", "arbitrary")),
        name="in_proj",
    )(x2, g, w_f32, *tables)


def _diff_attn_kernel(lq1_ref, lk1_ref, lq2_ref, lk2_ref, g_ref, q_ref, k_ref, v_ref, o_ref, vt_ref, s_ref):
    seq = k_ref.shape[0]
    tq = ATTN_TQ
    n_tiles = seq // tq
    n_chunks = seq // ATTN_KC
    fold = ATTN_KC // SUBLANES
    assert n_tiles % 2 == 0 and n_tiles >= 4

    vt_ref[...] = v_ref[...].T
    lam = (jnp.exp(jnp.sum(lq1_ref[...] * lk1_ref[...])) - jnp.exp(jnp.sum(lq2_ref[...] * lk2_ref[...]))
           + LAMBDA_INIT)
    comp1_lane = lax.broadcasted_iota(jnp.int32, (tq, HEAD_DIM), 1) < DIFF_HALF

    def load_q_parts(t):
        q = q_ref[pl.ds(pl.multiple_of(t * tq, tq), tq), :]
        zero = jnp.zeros_like(q)
        return jnp.where(comp1_lane, q, zero), jnp.where(comp1_lane, zero, q)

    def score_chunk(slot, c, q_parts, m8):
        ks = k_ref[c * ATTN_KC:(c + 1) * ATTN_KC, :]
        new_m8 = []
        for comp in range(2):
            s = lax.dot_general(ks, q_parts[comp], (((1,), (1,)), ((), ())), preferred_element_type=F32)
            s_ref[slot, comp, c * ATTN_KC:(c + 1) * ATTN_KC, :] = s
            cm = jnp.max(s.reshape(fold, SUBLANES, tq), axis=0)
            new_m8.append(cm if m8 is None else jnp.maximum(m8[comp], cm))
        return new_m8

    def prob_chunk(slot, c, m, acc):
        vts = vt_ref[:, c * ATTN_KC:(c + 1) * ATTN_KC]
        new_acc = []
        for comp in range(2):
            l8, o = acc[comp]
            p = jnp.exp2(s_ref[slot, comp, c * ATTN_KC:(c + 1) * ATTN_KC, :] - m[comp])
            l8 = l8 + jnp.sum(p.reshape(fold, SUBLANES, tq), axis=0)
            o = o + jnp.dot(vts, p.astype(BF16), preferred_element_type=F32)
            new_acc.append((l8, o))
        return new_acc

    def col_max(m8):
        return [jnp.max(x, axis=0, keepdims=True) for x in m8]

    def zero_acc():
        return [(jnp.zeros((SUBLANES, tq), F32), jnp.zeros((HEAD_DIM, tq), F32)) for _ in range(2)]

    def finish(t, acc):
        outs = [o / jnp.sum(l8, axis=0, keepdims=True) for l8, o in acc]
        out = (outs[0] - lam * outs[1]).T
        y = out * _rms_scale(out) * g_ref[...]
        o_ref[pl.ds(pl.multiple_of(t * tq, tq), tq), :] = (y * (1.0 - LAMBDA_INIT)).astype(o_ref.dtype)

    def stage(t_next, t_cur, m_cur, t_prev, acc_prev):
        q_parts = load_q_parts(t_next[0]) if t_next is not None else None
        m8, acc = None, zero_acc()
        for c in range(n_chunks):
            if t_next is not None:
                m8 = score_chunk(t_next[1], c, q_parts, m8)
            if t_cur is not None:
                acc = prob_chunk(t_cur[1], c, m_cur, acc)
            if c == 0 and t_prev is not None:
                finish(t_prev, acc_prev)
        return (col_max(m8) if m8 is not None else None), acc

    m_cur, _ = stage((0, 0), None, None, None, None)
    m_cur, acc = stage((1, 1), (0, 0), m_cur, None, None)

    def pair(u, carry):
        m_odd, acc_even = carry
        m_even, acc_odd = stage((2 * u, 0), (2 * u - 1, 1), m_odd, 2 * u - 2, acc_even)
        m_odd, acc_even = stage((2 * u + 1, 1), (2 * u, 0), m_even, 2 * u - 1, acc_odd)
        return m_odd, acc_even

    m_cur, acc = lax.fori_loop(1, n_tiles // 2, pair, (m_cur, acc))
    _, acc_last = stage(None, (n_tiles - 1, 1), m_cur, n_tiles - 2, acc)
    finish(n_tiles - 1, acc_last)


def _diff_attn(proj, lq1, lk1, lq2, lk2, subln_g, batch, seq):
    lam_spec = pl.BlockSpec((1, DIFF_HALF), lambda b, h: (0, 0))
    head_blk = lambda col0: pl.BlockSpec((seq, HEAD_DIM), lambda b, h: (b, col0 + h))
    return pl.pallas_call(
        _diff_attn_kernel,
        grid=(batch, N_HEADS),
        in_specs=[
            lam_spec, lam_spec, lam_spec, lam_spec,
            pl.BlockSpec((1, HEAD_DIM), lambda b, h: (0, 0)),
            head_blk(GROUP_DQ * N_HEADS), head_blk(GROUP_DK * N_HEADS), head_blk(GROUP_DV * N_HEADS),
        ],
        out_specs=pl.BlockSpec((seq, HEAD_DIM), lambda b, h: (b, h)),
        out_shape=jax.ShapeDtypeStruct((batch * seq, GROUP_WIDTH), BF16),
        scratch_shapes=[pltpu.VMEM((HEAD_DIM, seq), BF16), pltpu.VMEM((2, 2, seq, ATTN_TQ), F32)],
        compiler_params=_compiler_params(("arbitrary", "arbitrary")),
        name="diff_attn",
    )(lq1, lk1, lq2, lk2, subln_g, proj, proj, proj)


N_REL_ROWS = 2 * WIN_H - 1
N_REL_COLS = 2 * WIN_W - 1
N_PAIR_TABLES = N_REL_ROWS - 1


def _na_kernel(rb_ref, q_ref, k_ref, v_ref, o_ref, tab_ref, *, batch):
    head = pl.program_id(0)
    rows = q_ref.shape[0] // (batch * GRID_W)
    band = WIN_H * GRID_W

    qc = lax.broadcasted_iota(jnp.int32, (GRID_W, LANES), 0)
    ln = lax.broadcasted_iota(jnp.int32, (GRID_W, LANES), 1)
    kc = ln & (GRID_W - 1)
    upper = ln >= GRID_W
    rel = jnp.clip(kc - qc + (WIN_W - 1), 0, N_REL_COLS - 1)
    col_start = jnp.clip(qc - WIN_W // 2, 0, GRID_W - WIN_W)
    inside = (kc >= col_start) & (kc < col_start + WIN_W)
    base = head * (N_REL_ROWS * N_REL_COLS)

    def build(t, carry):
        acc = jnp.zeros((GRID_W, LANES), F32)
        for d in range(N_REL_COLS):
            lo = rb_ref[base + t * N_REL_COLS + d]
            hi = rb_ref[base + (t + 1) * N_REL_COLS + d]
            acc = jnp.where(rel == d, jnp.where(upper, hi, lo), acc)
        tab_ref[t] = jnp.where(inside, acc * LOG2_E, NEG_INF)
        return carry

    lax.fori_loop(0, N_PAIR_TABLES, build, 0)

    def band_start(r):
        return min(max(r - WIN_H // 2, 0), rows - WIN_H)

    def token_rows(b, r, n_rows):
        start = (b * rows + r) * GRID_W
        return slice(start, start + n_rows * GRID_W)

    def score(b, r):
        q = q_ref[token_rows(b, r, 1), :]
        kb = k_ref[token_rows(b, band_start(r), WIN_H), :]
        return lax.dot_general(q, kb, (((1,), (1,)), ((), ())), preferred_element_type=F32)

    def softmax(r, s):
        rel0 = band_start(r) - r + (WIN_H - 1)
        parts = [s[:, jj * LANES:(jj + 1) * LANES] + tab_ref[rel0 + 2 * jj] for jj in range(band // LANES)]
        logits = jnp.concatenate(parts, axis=-1)
        p = jnp.exp2(logits - jnp.max(logits, axis=-1, keepdims=True))
        return p.astype(BF16), jnp.sum(p, axis=-1, keepdims=True)

    def value(b, r, p, l):
        vb = v_ref[token_rows(b, band_start(r), WIN_H), :]
        o_ref[token_rows(b, r, 1), :] = (jnp.dot(p, vb, preferred_element_type=F32) / l).astype(o_ref.dtype)

    units = [(b, r) for b in range(batch) for r in range(rows)]
    groups = [units[g:g + NA_ROWS_PER_GROUP] for g in range(0, len(units), NA_ROWS_PER_GROUP)]
    scores = {u: score(*u) for u in groups[0]}
    for gi, group in enumerate(groups):
        probs = {u: softmax(u[1], scores.pop(u)) for u in group}
        nxt = groups[gi + 1] if gi + 1 < len(groups) else ()
        for i, u in enumerate(group):
            if i < len(nxt):
                scores[nxt[i]] = score(*nxt[i])
            value(*u, *probs[u])


def _na_attn(proj, rel_bias_flat, batch, seq):
    blk = lambda group: pl.BlockSpec((batch * seq, HEAD_DIM), lambda h: (0, group * N_HEADS + h))
    return pl.pallas_call(
        functools.partial(_na_kernel, batch=batch),
        grid=(N_HEADS,),
        in_specs=[
            pl.BlockSpec(memory_space=pltpu.SMEM),
            blk(GROUP_NQ), blk(GROUP_NK), blk(GROUP_NV),
        ],
        out_specs=pl.BlockSpec((batch * seq, HEAD_DIM), lambda h: (0, h)),
        out_shape=jax.ShapeDtypeStruct((batch * seq, GROUP_WIDTH), BF16),
        scratch_shapes=[pltpu.VMEM((N_PAIR_TABLES, GRID_W, LANES), F32)],
        compiler_params=_compiler_params(("arbitrary",)),
        name="na_attn",
    )(rel_bias_flat, proj, proj, proj)


def _out_proj_kernel(x_ref, a_ref, n_ref, wa_ref, wn_ref, g_ref, x1_ref, h_ref, wa_bf_ref, wn_bf_ref):
    @pl.when(pl.program_id(0) == 0)
    def _():
        wa_bf_ref[...] = wa_ref[...].astype(BF16)
        wn_bf_ref[...] = wn_ref[...].astype(BF16)

    for mc in range(x_ref.shape[0] // OUT_MC):
        rows = slice(mc * OUT_MC, (mc + 1) * OUT_MC)
        acc = jnp.dot(a_ref[rows, :], wa_bf_ref[...], preferred_element_type=F32)
        acc = acc + jnp.dot(n_ref[rows, :], wn_bf_ref[...], preferred_element_type=F32)
        x1 = x_ref[rows, :] + acc
        x1_ref[rows, :] = x1
        h_ref[rows, :] = (x1 * _rms_scale(x1) * g_ref[...]).astype(BF16)


def _out_proj(x2, a_out, n_out, w_out_f32, g_mlp):
    rows = x2.shape[0]
    tm = OUT_TM
    row_blk = lambda width: pl.BlockSpec((tm, width), lambda i: (i, 0))
    w_half = lambda half: pl.BlockSpec((GROUP_WIDTH, D_MODEL), lambda i: (half, 0), pipeline_mode=pl.Buffered(1))
    return pl.pallas_call(
        _out_proj_kernel,
        grid=(rows // tm,),
        in_specs=[
            row_blk(D_MODEL), row_blk(GROUP_WIDTH), row_blk(GROUP_WIDTH),
            w_half(0), w_half(1),
            pl.BlockSpec((1, D_MODEL), lambda i: (0, 0)),
        ],
        out_specs=[row_blk(D_MODEL), row_blk(D_MODEL)],
        out_shape=[jax.ShapeDtypeStruct((rows, D_MODEL), F32), jax.ShapeDtypeStruct((rows, D_MODEL), BF16)],
        scratch_shapes=[pltpu.VMEM((GROUP_WIDTH, D_MODEL), BF16), pltpu.VMEM((GROUP_WIDTH, D_MODEL), BF16)],
        compiler_params=_compiler_params(("arbitrary",)),
        name="out_proj",
    )(x2, a_out, n_out, w_out_f32, w_out_f32, g_mlp)


def _mlp_kernel(h_ref, x1_ref, wu_ref, wd_ref, gf_ref, o_ref):
    j = pl.program_id(1)
    last_j = pl.num_programs(1) - 1
    tm = o_ref.shape[0]
    piece = x1_ref.shape[0]
    chunks = [(start, start + MLP_MC) for start in range(0, tm, MLP_MC)]
    n_chunks = len(chunks)

    def step(first, last):
        wu = wu_ref[...].astype(BF16)
        wd = wd_ref[...].astype(BF16)

        def up(c):
            u = jnp.dot(h_ref[slice(*chunks[c]), :], wu, preferred_element_type=F32)
            r = jnp.maximum(u, 0.0)
            return (r * r).astype(BF16)

        def down(c, act):
            rows = slice(*chunks[c])
            y = jnp.dot(act, wd, preferred_element_type=F32)
            if not first:
                y = y + o_ref[rows, :]
            if last:
                y = y * _rms_scale(y) * gf_ref[...]
            o_ref[rows, :] = y

        if last:
            o_ref[tm - piece:tm, :] += x1_ref[...]
        act = up(0)
        for c in range(n_chunks):
            nxt = up(c + 1) if c + 1 < n_chunks else None
            down(c, act)
            act = nxt
        if first:
            o_ref[0:piece, :] += x1_ref[...]
        elif not last:
            o_ref[pl.ds(pl.multiple_of(j * piece, piece), piece), :] += x1_ref[...]

    pl.when(j == 0)(functools.partial(step, True, False))
    pl.when((j > 0) & (j < last_j))(functools.partial(step, False, False))
    pl.when(j == last_j)(functools.partial(step, False, True))


def _mlp(h, x1, w_up_f32, w_down_f32, g_final):
    rows = x1.shape[0]
    tm, tf = MLP_TM, MLP_TF
    steps = D_FF // tf
    piece = tm // steps
    return pl.pallas_call(
        _mlp_kernel,
        grid=(rows // tm, steps),
        in_specs=[
            pl.BlockSpec((tm, D_MODEL), lambda i, j: (i, 0)),
            pl.BlockSpec((piece, D_MODEL), lambda i, j: (i * steps + j, 0)),
            pl.BlockSpec((D_MODEL, tf), lambda i, j: (0, j)),
            pl.BlockSpec((tf, D_MODEL), lambda i, j: (j, 0)),
            pl.BlockSpec((1, D_MODEL), lambda i, j: (0, 0)),
        ],
        out_specs=pl.BlockSpec((tm, D_MODEL), lambda i, j: (i, 0)),
        out_shape=jax.ShapeDtypeStruct((rows, D_MODEL), F32),
        compiler_params=_compiler_params(("arbitrary", "arbitrary")),
        name="mlp",
    )(h, x1, w_up_f32, w_down_f32, g_final)


def kernel(x, norm_mix_g, w_in, lambda_q1, lambda_k1, lambda_q2, lambda_k2, diff_subln_g, na_rel_bias, w_out,
           norm_mlp_g, w_up, w_down, norm_final_g):
    batch, seq, d_model = x.shape
    assert d_model == D_MODEL and w_in.shape == (1, D_MODEL, N_GROUPS * GROUP_WIDTH)
    assert seq % PROJ_TM == 0 and seq % ATTN_TQ == 0 and seq % ATTN_KC == 0 and seq % (GRID_W * WIN_H) == 0
    x2 = x.reshape(batch * seq, D_MODEL)

    proj = _in_proj(x2, norm_mix_g, w_in[0], seq)
    a_out = _diff_attn(proj, lambda_q1, lambda_k1, lambda_q2, lambda_k2, diff_subln_g, batch, seq)
    n_out = _na_attn(proj, na_rel_bias[0].reshape(-1), batch, seq)
    x1, h_mlp = _out_proj(x2, a_out, n_out, w_out[0], norm_mlp_g)
    y = _mlp(h_mlp, x1, w_up[0], w_down[0], norm_final_g.reshape(1, D_MODEL))
    return y.reshape(batch, seq, D_MODEL)
```

```python
import functools
import math

import jax
import jax.numpy as jnp
import numpy as np
from jax import lax
from jax.experimental import pallas as pl
from jax.experimental.pallas import tpu as pltpu

F32 = jnp.float32
BF16 = jnp.bfloat16

D_MODEL = 2048
HEAD_DIM = 128
N_HEADS = 8
GROUP_WIDTH = N_HEADS * HEAD_DIM
N_GROUPS = 6
GROUP_DQ, GROUP_DK, GROUP_DV, GROUP_NQ, GROUP_NK, GROUP_NV = range(N_GROUPS)
DIFF_HALF = HEAD_DIM // 2
ROT_DIM = DIFF_HALF // 4
ROT_HALF = ROT_DIM // 2
ROPE_THETA = 500000.0
GRID_W = 64
WIN_H = 8
WIN_W = 16
D_FF = 4 * D_MODEL
EPS = 1e-5
NEG_INF = -1e30
LAMBDA_INIT = 0.8 - 0.6 * math.exp(-0.3 * 0)

LANES = 128
SUBLANES = 8
VMEM_LIMIT_BYTES = 56 * 1024 * 1024

PROJ_TM = 1024
PROJ_MC = 256
ATTN_TQ = 256
ATTN_KC = 512
OUT_TM = 512
OUT_MC = 256
MLP_TM = 1024
MLP_TF = 512
MLP_MC = 512
NA_ROWS_PER_GROUP = 16


def _compiler_params(semantics):
    return pltpu.CompilerParams(dimension_semantics=semantics, vmem_limit_bytes=VMEM_LIMIT_BYTES)


def _row_chunks(total, chunk):
    bounds = list(range(0, total, chunk)) + [total]
    bounds.insert(-1, bounds[-2] + chunk // 2)
    return list(zip(bounds[:-1], bounds[1:]))


def _rms_scale(x):
    return lax.rsqrt(jnp.mean(x * x, axis=-1, keepdims=True) + EPS)


LOG2_E = math.log2(math.e)
SCORE_SCALE_LOG2 = LOG2_E / math.sqrt(DIFF_HALF)
NA_SCALE_LOG2 = LOG2_E / math.sqrt(HEAD_DIM)


def _in_proj_kernel(x_ref, g_ref, w_ref, c_ref, s_lo_ref, s_hi_ref, o_ref, h_ref):
    j = pl.program_id(1)

    def step(first, rotary):
        if rotary:
            q_scale = SCORE_SCALE_LOG2 if first else 1.0
        else:
            q_scale = jnp.where(j == GROUP_NQ, NA_SCALE_LOG2, 1.0).astype(F32)
        w = w_ref[...].astype(BF16)
        for start, stop in _row_chunks(x_ref.shape[0], PROJ_MC):
            rows = slice(start, stop)
            if first:
                x = x_ref[rows, :]
                h_ref[rows, :] = (x * _rms_scale(x) * g_ref[...]).astype(BF16)
            acc = jnp.dot(h_ref[rows, :], w, preferred_element_type=F32)
            if not rotary:
                o_ref[rows, :] = (acc * q_scale).astype(BF16)
                continue
            c = c_ref[rows, :] * q_scale
            s_lo = s_lo_ref[rows, :] * q_scale
            s_hi = s_hi_ref[rows, :] * q_scale
            for hh in range(N_HEADS):
                a = acc[:, hh * HEAD_DIM:(hh + 1) * HEAD_DIM]
                r = a * c + pltpu.roll(a, LANES - ROT_HALF, 1) * s_lo + pltpu.roll(a, ROT_HALF, 1) * s_hi
                o_ref[rows, hh * HEAD_DIM:(hh + 1) * HEAD_DIM] = r.astype(BF16)

    assert GROUP_DQ == 0
    pl.when(j == GROUP_DQ)(functools.partial(step, True, True))
    pl.when(j == GROUP_DK)(functools.partial(step, False, True))
    pl.when((j != GROUP_DQ) & (j != GROUP_DK))(functools.partial(step, False, False))


@functools.lru_cache(maxsize=None)
def _rotary_lane_tables(seq):
    lane = np.arange(LANES)
    inv_freq = np.power(np.float32(ROPE_THETA), -np.arange(0, ROT_DIM, 2, dtype=np.float32) / np.float32(ROT_DIM))
    ang = (np.arange(seq, dtype=np.float32)[:, None] * inv_freq[lane % ROT_HALF][None, :]).astype(np.float32)
    in_lo = (lane % DIFF_HALF) < ROT_HALF
    in_hi = ((lane % DIFF_HALF) >= ROT_HALF) & ((lane % DIFF_HALF) < ROT_DIM)
    cos, sin = np.cos(ang).astype(np.float32), np.sin(ang).astype(np.float32)
    c = np.where(in_lo | in_hi, cos, np.float32(1.0))
    s_lo = np.where(in_lo, -sin, np.float32(0.0))
    s_hi = np.where(in_hi, sin, np.float32(0.0))
    return c, s_lo, s_hi


def _in_proj(x2, g, w_f32, seq):
    rows = x2.shape[0]
    tm = PROJ_TM
    tables = [jnp.asarray(t) for t in _rotary_lane_tables(seq)]
    pos_blocks = seq // tm
    tab_spec = pl.BlockSpec((tm, LANES), lambda i, j: (i % pos_blocks, 0))
    return pl.pallas_call(
        _in_proj_kernel,
        grid=(rows // tm, N_GROUPS),
        in_specs=[
            pl.BlockSpec((tm, D_MODEL), lambda i, j: (i, 0)),
            pl.BlockSpec((1, D_MODEL), lambda i, j: (0, 0)),
            pl.BlockSpec((D_MODEL, GROUP_WIDTH), lambda i, j: (0, j)),
            tab_spec, tab_spec, tab_spec,
        ],
        out_specs=pl.BlockSpec((tm, GROUP_WIDTH), lambda i, j: (i, j)),
        out_shape=jax.ShapeDtypeStruct((rows, N_GROUPS * GROUP_WIDTH), BF16),
        scratch_shapes=[pltpu.VMEM((tm, D_MODEL), BF16)],
        compiler_params=_compiler_params(("arbitrary", "arbitrary")),
        name="in_proj",
    )(x2, g, w_f32, *tables)


def _diff_attn_kernel(lq1_ref, lk1_ref, lq2_ref, lk2_ref, g_ref, q_ref, k_ref, v_ref, o_ref, vt_ref, s_ref):
    seq = k_ref.shape[0]
    tq = ATTN_TQ
    n_tiles = seq // tq
    n_chunks = seq // ATTN_KC
    fold = ATTN_KC // SUBLANES
    assert n_tiles % 2 == 0 and n_tiles >= 4

    vt_ref[...] = v_ref[...].T
    lam = (jnp.exp(jnp.sum(lq1_ref[...] * lk1_ref[...])) - jnp.exp(jnp.sum(lq2_ref[...] * lk2_ref[...]))
           + LAMBDA_INIT)
    comp1_lane = lax.broadcasted_iota(jnp.int32, (tq, HEAD_DIM), 1) < DIFF_HALF

    def load_q_parts(t):
        q = q_ref[pl.ds(pl.multiple_of(t * tq, tq), tq), :]
        zero = jnp.zeros_like(q)
        return jnp.where(comp1_lane, q, zero), jnp.where(comp1_lane, zero, q)

    def score_chunk(slot, c, q_parts, m8):
        ks = k_ref[c * ATTN_KC:(c + 1) * ATTN_KC, :]
        new_m8 = []
        for comp in range(2):
            s = lax.dot_general(ks, q_parts[comp], (((1,), (1,)), ((), ())), preferred_element_type=F32)
            s_ref[slot, comp, c * ATTN_KC:(c + 1) * ATTN_KC, :] = s
            cm = jnp.max(s.reshape(fold, SUBLANES, tq), axis=0)
            new_m8.append(cm if m8 is None else jnp.maximum(m8[comp], cm))
        return new_m8

    def prob_chunk(slot, c, m, acc):
        vts = vt_ref[:, c * ATTN_KC:(c + 1) * ATTN_KC]
        new_acc = []
        for comp in range(2):
            l8, o = acc[comp]
            p = jnp.exp2(s_ref[slot, comp, c * ATTN_KC:(c + 1) * ATTN_KC, :] - m[comp])
            l8 = l8 + jnp.sum(p.reshape(fold, SUBLANES, tq), axis=0)
            o = o + jnp.dot(vts, p.astype(BF16), preferred_element_type=F32)
            new_acc.append((l8, o))
        return new_acc

    def col_max(m8):
        return [jnp.max(x, axis=0, keepdims=True) for x in m8]

    def zero_acc():
        return [(jnp.zeros((SUBLANES, tq), F32), jnp.zeros((HEAD_DIM, tq), F32)) for _ in range(2)]

    def finish(t, acc):
        outs = [o / jnp.sum(l8, axis=0, keepdims=True) for l8, o in acc]
        out = (outs[0] - lam * outs[1]).T
        y = out * _rms_scale(out) * g_ref[...]
        o_ref[pl.ds(pl.multiple_of(t * tq, tq), tq), :] = (y * (1.0 - LAMBDA_INIT)).astype(o_ref.dtype)

    def stage(t_next, t_cur, m_cur, t_prev, acc_prev):
        q_parts = load_q_parts(t_next[0]) if t_next is not None else None
        m8, acc = None, zero_acc()
        for c in range(n_chunks):
            if t_next is not None:
                m8 = score_chunk(t_next[1], c, q_parts, m8)
            if t_cur is not None:
                acc = prob_chunk(t_cur[1], c, m_cur, acc)
            if c == 0 and t_prev is not None:
                finish(t_prev, acc_prev)
        return (col_max(m8) if m8 is not None else None), acc

    m_cur, _ = stage((0, 0), None, None, None, None)
    m_cur, acc = stage((1, 1), (0, 0), m_cur, None, None)

    def pair(u, carry):
        m_odd, acc_even = carry
        m_even, acc_odd = stage((2 * u, 0), (2 * u - 1, 1), m_odd, 2 * u - 2, acc_even)
        m_odd, acc_even = stage((2 * u + 1, 1), (2 * u, 0), m_even, 2 * u - 1, acc_odd)
        return m_odd, acc_even

    m_cur, acc = lax.fori_loop(1, n_tiles // 2, pair, (m_cur, acc))
    _, acc_last = stage(None, (n_tiles - 1, 1), m_cur, n_tiles - 2, acc)
    finish(n_tiles - 1, acc_last)


def _diff_attn(proj, lq1, lk1, lq2, lk2, subln_g, batch, seq):
    lam_spec = pl.BlockSpec((1, DIFF_HALF), lambda b, h: (0, 0))
    head_blk = lambda col0: pl.BlockSpec((seq, HEAD_DIM), lambda b, h: (b, col0 + h))
    return pl.pallas_call(
        _diff_attn_kernel,
        grid=(batch, N_HEADS),
        in_specs=[
            lam_spec, lam_spec, lam_spec, lam_spec,
            pl.BlockSpec((1, HEAD_DIM), lambda b, h: (0, 0)),
            head_blk(GROUP_DQ * N_HEADS), head_blk(GROUP_DK * N_HEADS), head_blk(GROUP_DV * N_HEADS),
        ],
        out_specs=pl.BlockSpec((seq, HEAD_DIM), lambda b, h: (b, h)),
        out_shape=jax.ShapeDtypeStruct((batch * seq, GROUP_WIDTH), BF16),
        scratch_shapes=[pltpu.VMEM((HEAD_DIM, seq), BF16), pltpu.VMEM((2, 2, seq, ATTN_TQ), F32)],
        compiler_params=_compiler_params(("arbitrary", "arbitrary")),
        name="diff_attn",
    )(lq1, lk1, lq2, lk2, subln_g, proj, proj, proj)


N_REL_ROWS = 2 * WIN_H - 1
N_REL_COLS = 2 * WIN_W - 1
N_PAIR_TABLES = N_REL_ROWS - 1


def _na_kernel(rb_ref, q_ref, k_ref, v_ref, o_ref, tab_ref, *, batch):
    head = pl.program_id(0)
    rows = q_ref.shape[0] // (batch * GRID_W)
    band = WIN_H * GRID_W

    qc = lax.broadcasted_iota(jnp.int32, (GRID_W, LANES), 0)
    ln = lax.broadcasted_iota(jnp.int32, (GRID_W, LANES), 1)
    kc = ln & (GRID_W - 1)
    upper = ln >= GRID_W
    rel = jnp.clip(kc - qc + (WIN_W - 1), 0, N_REL_COLS - 1)
    col_start = jnp.clip(qc - WIN_W // 2, 0, GRID_W - WIN_W)
    inside = (kc >= col_start) & (kc < col_start + WIN_W)
    base = head * (N_REL_ROWS * N_REL_COLS)

    def build(t, carry):
        acc = jnp.zeros((GRID_W, LANES), F32)
        for d in range(N_REL_COLS):
            lo = rb_ref[base + t * N_REL_COLS + d]
            hi = rb_ref[base + (t + 1) * N_REL_COLS + d]
            acc = jnp.where(rel == d, jnp.where(upper, hi, lo), acc)
        tab_ref[t] = jnp.where(inside, acc * LOG2_E, NEG_INF)
        return carry

    lax.fori_loop(0, N_PAIR_TABLES, build, 0)

    def band_start(r):
        return min(max(r - WIN_H // 2, 0), rows - WIN_H)

    def token_rows(b, r, n_rows):
        start = (b * rows + r) * GRID_W
        return slice(start, start + n_rows * GRID_W)

    def score(b, r):
        q = q_ref[token_rows(b, r, 1), :]
        kb = k_ref[token_rows(b, band_start(r), WIN_H), :]
        return lax.dot_general(q, kb, (((1,), (1,)), ((), ())), preferred_element_type=F32)

    def softmax(r, s):
        rel0 = band_start(r) - r + (WIN_H - 1)
        parts = [s[:, jj * LANES:(jj + 1) * LANES] + tab_ref[rel0 + 2 * jj] for jj in range(band // LANES)]
        logits = jnp.concatenate(parts, axis=-1)
        p = jnp.exp2(logits - jnp.max(logits, axis=-1, keepdims=True))
        return p.astype(BF16), jnp.sum(p, axis=-1, keepdims=True)

    def value(b, r, p, l):
        vb = v_ref[token_rows(b, band_start(r), WIN_H), :]
        o_ref[token_rows(b, r, 1), :] = (jnp.dot(p, vb, preferred_element_type=F32) / l).astype(o_ref.dtype)

    units = [(b, r) for b in range(batch) for r in range(rows)]
    groups = [units[g:g + NA_ROWS_PER_GROUP] for g in range(0, len(units), NA_ROWS_PER_GROUP)]
    scores = {u: score(*u) for u in groups[0]}
    for gi, group in enumerate(groups):
        probs = {u: softmax(u[1], scores.pop(u)) for u in group}
        nxt = groups[gi + 1] if gi + 1 < len(groups) else ()
        for i, u in enumerate(group):
            if i < len(nxt):
                scores[nxt[i]] = score(*nxt[i])
            value(*u, *probs[u])


def _na_attn(proj, rel_bias_flat, batch, seq):
    blk = lambda group: pl.BlockSpec((batch * seq, HEAD_DIM), lambda h: (0, group * N_HEADS + h))
    return pl.pallas_call(
        functools.partial(_na_kernel, batch=batch),
        grid=(N_HEADS,),
        in_specs=[
            pl.BlockSpec(memory_space=pltpu.SMEM),
            blk(GROUP_NQ), blk(GROUP_NK), blk(GROUP_NV),
        ],
        out_specs=pl.BlockSpec((batch * seq, HEAD_DIM), lambda h: (0, h)),
        out_shape=jax.ShapeDtypeStruct((batch * seq, GROUP_WIDTH), BF16),
        scratch_shapes=[pltpu.VMEM((N_PAIR_TABLES, GRID_W, LANES), F32)],
        compiler_params=_compiler_params(("arbitrary",)),
        name="na_attn",
    )(rel_bias_flat, proj, proj, proj)


def _out_proj_kernel(x_ref, a_ref, n_ref, wa_ref, wn_ref, g_ref, x1_ref, h_ref, wa_bf_ref, wn_bf_ref):
    @pl.when(pl.program_id(0) == 0)
    def _():
        wa_bf_ref[...] = wa_ref[...].astype(BF16)
        wn_bf_ref[...] = wn_ref[...].astype(BF16)

    for mc in range(x_ref.shape[0] // OUT_MC):
        rows = slice(mc * OUT_MC, (mc + 1) * OUT_MC)
        acc = jnp.dot(a_ref[rows, :], wa_bf_ref[...], preferred_element_type=F32)
        acc = acc + jnp.dot(n_ref[rows, :], wn_bf_ref[...], preferred_element_type=F32)
        x1 = x_ref[rows, :] + acc
        x1_ref[rows, :] = x1
        h_ref[rows, :] = (x1 * _rms_scale(x1) * g_ref[...]).astype(BF16)


def _out_proj(x2, a_out, n_out, w_out_f32, g_mlp):
    rows = x2.shape[0]
    tm = OUT_TM
    row_blk = lambda width: pl.BlockSpec((tm, width), lambda i: (i, 0))
    w_half = lambda half: pl.BlockSpec((GROUP_WIDTH, D_MODEL), lambda i: (half, 0), pipeline_mode=pl.Buffered(1))
    return pl.pallas_call(
        _out_proj_kernel,
        grid=(rows // tm,),
        in_specs=[
            row_blk(D_MODEL), row_blk(GROUP_WIDTH), row_blk(GROUP_WIDTH),
            w_half(0), w_half(1),
            pl.BlockSpec((1, D_MODEL), lambda i: (0, 0)),
        ],
        out_specs=[row_blk(D_MODEL), row_blk(D_MODEL)],
        out_shape=[jax.ShapeDtypeStruct((rows, D_MODEL), F32), jax.ShapeDtypeStruct((rows, D_MODEL), BF16)],
        scratch_shapes=[pltpu.VMEM((GROUP_WIDTH, D_MODEL), BF16), pltpu.VMEM((GROUP_WIDTH, D_MODEL), BF16)],
        compiler_params=_compiler_params(("arbitrary",)),
        name="out_proj",
    )(x2, a_out, n_out, w_out_f32, w_out_f32, g_mlp)


def _mlp_kernel(h_ref, x1_ref, wu_ref, wd_ref, gf_ref, o_ref):
    j = pl.program_id(1)
    last_j = pl.num_programs(1) - 1
    tm = o_ref.shape[0]
    piece = x1_ref.shape[0]
    chunks = [(start, start + MLP_MC) for start in range(0, tm, MLP_MC)]
    n_chunks = len(chunks)

    def step(first, last):
        wu = wu_ref[...].astype(BF16)
        wd = wd_ref[...].astype(BF16)

        def up(c):
            u = jnp.dot(h_ref[slice(*chunks[c]), :], wu, preferred_element_type=F32)
            r = jnp.maximum(u, 0.0)
            return (r * r).astype(BF16)

        def down(c, act):
            rows = slice(*chunks[c])
            y = jnp.dot(act, wd, preferred_element_type=F32)
            if not first:
                y = y + o_ref[rows, :]
            if last:
                y = y * _rms_scale(y) * gf_ref[...]
            o_ref[rows, :] = y

        if last:
            o_ref[tm - piece:tm, :] += x1_ref[...]
        act = up(0)
        for c in range(n_chunks):
            nxt = up(c + 1) if c + 1 < n_chunks else None
            down(c, act)
            act = nxt
        if first:
            o_ref[0:piece, :] += x1_ref[...]
        elif not last:
            o_ref[pl.ds(pl.multiple_of(j * piece, piece), piece), :] += x1_ref[...]

    pl.when(j == 0)(functools.partial(step, True, False))
    pl.when((j > 0) & (j < last_j))(functools.partial(step, False, False))
    pl.when(j == last_j)(functools.partial(step, False, True))


def _mlp(h, x1, w_up_f32, w_down_f32, g_final):
    rows = x1.shape[0]
    tm, tf = MLP_TM, MLP_TF
    steps = D_FF // tf
    piece = tm // steps
    return pl.pallas_call(
        _mlp_kernel,
        grid=(rows // tm, steps),
        in_specs=[
            pl.BlockSpec((tm, D_MODEL), lambda i, j: (i, 0)),
            pl.BlockSpec((piece, D_MODEL), lambda i, j: (i * steps + j, 0)),
            pl.BlockSpec((D_MODEL, tf), lambda i, j: (0, j)),
            pl.BlockSpec((tf, D_MODEL), lambda i, j: (j, 0)),
            pl.BlockSpec((1, D_MODEL), lambda i, j: (0, 0)),
        ],
        out_specs=pl.BlockSpec((tm, D_MODEL), lambda i, j: (i, 0)),
        out_shape=jax.ShapeDtypeStruct((rows, D_MODEL), F32),
        compiler_params=_compiler_params(("arbitrary", "arbitrary")),
        name="mlp",
    )(h, x1, w_up_f32, w_down_f32, g_final)


def kernel(x, norm_mix_g, w_in, lambda_q1, lambda_k1, lambda_q2, lambda_k2, diff_subln_g, na_rel_bias, w_out,
           norm_mlp_g, w_up, w_down, norm_final_g):
    batch, seq, d_model = x.shape
    assert d_model == D_MODEL and w_in.shape == (1, D_MODEL, N_GROUPS * GROUP_WIDTH)
    assert seq % PROJ_TM == 0 and seq % ATTN_TQ == 0 and seq % ATTN_KC == 0 and seq % (GRID_W * WIN_H) == 0
    x2 = x.reshape(batch * seq, D_MODEL)

    proj = _in_proj(x2, norm_mix_g, w_in[0], seq)
    a_out = _diff_attn(proj, lambda_q1, lambda_k1, lambda_q2, lambda_k2, diff_subln_g, batch, seq)
    n_out = _na_attn(proj, na_rel_bias[0].reshape(-1), batch, seq)
    x1, h_mlp = _out_proj(x2, a_out, n_out, w_out[0], norm_mlp_g)
    y = _mlp(h_mlp, x1, w_up[0], w_down[0], norm_final_g.reshape(1, D_MODEL))
    return y.reshape(batch, seq, D_MODEL)
```

```python
import functools
import math

import jax
import jax.numpy as jnp
import numpy as np
from jax import lax
from jax.experimental import pallas as pl
from jax.experimental.pallas import tpu as pltpu

F32 = jnp.float32
BF16 = jnp.bfloat16

D_MODEL = 2048
HEAD_DIM = 128
N_HEADS = 8
GROUP_WIDTH = N_HEADS * HEAD_DIM
N_GROUPS = 6
GROUP_DQ, GROUP_DK, GROUP_DV, GROUP_NQ, GROUP_NK, GROUP_NV = range(N_GROUPS)
DIFF_HALF = HEAD_DIM // 2
ROT_DIM = DIFF_HALF // 4
ROT_HALF = ROT_DIM // 2
ROPE_THETA = 500000.0
GRID_W = 64
WIN_H = 8
WIN_W = 16
D_FF = 4 * D_MODEL
EPS = 1e-5
NEG_INF = -1e30
LAMBDA_INIT = 0.8 - 0.6 * math.exp(-0.3 * 0)

LANES = 128
SUBLANES = 8
VMEM_LIMIT_BYTES = 56 * 1024 * 1024

PROJ_TM = 1024
PROJ_MC = 256
ATTN_TQ = 256
ATTN_KC = 1024
OUT_TM = 512
OUT_MC = 256
MLP_TM = 1024
MLP_TF = 512
MLP_MC = 512
NA_ROWS_PER_GROUP = 16


def _compiler_params(semantics):
    return pltpu.CompilerParams(dimension_semantics=semantics, vmem_limit_bytes=VMEM_LIMIT_BYTES)


def _row_chunks(total, chunk):
    bounds = list(range(0, total, chunk)) + [total]
    bounds.insert(-1, bounds[-2] + chunk // 2)
    return list(zip(bounds[:-1], bounds[1:]))


def _rms_scale(x):
    return lax.rsqrt(jnp.mean(x * x, axis=-1, keepdims=True) + EPS)


LOG2_E = math.log2(math.e)
SCORE_SCALE_LOG2 = LOG2_E / math.sqrt(DIFF_HALF)
NA_SCALE_LOG2 = LOG2_E / math.sqrt(HEAD_DIM)


def _in_proj_kernel(x_ref, g_ref, w_ref, c_ref, s_lo_ref, s_hi_ref, o_ref, h_ref):
    j = pl.program_id(1)

    def step(first, rotary):
        if rotary:
            q_scale = SCORE_SCALE_LOG2 if first else 1.0
        else:
            q_scale = jnp.where(j == GROUP_NQ, NA_SCALE_LOG2, 1.0).astype(F32)
        w = w_ref[...].astype(BF16)
        for start, stop in _row_chunks(x_ref.shape[0], PROJ_MC):
            rows = slice(start, stop)
            if first:
                x = x_ref[rows, :]
                h_ref[rows, :] = (x * _rms_scale(x) * g_ref[...]).astype(BF16)
            acc = jnp.dot(h_ref[rows, :], w, preferred_element_type=F32)
            if not rotary:
                o_ref[rows, :] = (acc * q_scale).astype(BF16)
                continue
            c = c_ref[rows, :] * q_scale
            s_lo = s_lo_ref[rows, :] * q_scale
            s_hi = s_hi_ref[rows, :] * q_scale
            for hh in range(N_HEADS):
                a = acc[:, hh * HEAD_DIM:(hh + 1) * HEAD_DIM]
                r = a * c + pltpu.roll(a, LANES - ROT_HALF, 1) * s_lo + pltpu.roll(a, ROT_HALF, 1) * s_hi
                o_ref[rows, hh * HEAD_DIM:(hh + 1) * HEAD_DIM] = r.astype(BF16)

    assert GROUP_DQ == 0
    pl.when(j == GROUP_DQ)(functools.partial(step, True, True))
    pl.when(j == GROUP_DK)(functools.partial(step, False, True))
    pl.when((j != GROUP_DQ) & (j != GROUP_DK))(functools.partial(step, False, False))


@functools.lru_cache(maxsize=None)
def _rotary_lane_tables(seq):
    lane = np.arange(LANES)
    inv_freq = np.power(np.float32(ROPE_THETA), -np.arange(0, ROT_DIM, 2, dtype=np.float32) / np.float32(ROT_DIM))
    ang = (np.arange(seq, dtype=np.float32)[:, None] * inv_freq[lane % ROT_HALF][None, :]).astype(np.float32)
    in_lo = (lane % DIFF_HALF) < ROT_HALF
    in_hi = ((lane % DIFF_HALF) >= ROT_HALF) & ((lane % DIFF_HALF) < ROT_DIM)
    cos, sin = np.cos(ang).astype(np.float32), np.sin(ang).astype(np.float32)
    c = np.where(in_lo | in_hi, cos, np.float32(1.0))
    s_lo = np.where(in_lo, -sin, np.float32(0.0))
    s_hi = np.where(in_hi, sin, np.float32(0.0))
    return c, s_lo, s_hi


def _in_proj(x2, g, w_f32, seq):
    rows = x2.shape[0]
    tm = PROJ_TM
    tables = [jnp.asarray(t) for t in _rotary_lane_tables(seq)]
    pos_blocks = seq // tm
    tab_spec = pl.BlockSpec((tm, LANES), lambda i, j: (i % pos_blocks, 0))
    return pl.pallas_call(
        _in_proj_kernel,
        grid=(rows // tm, N_GROUPS),
        in_specs=[
            pl.BlockSpec((tm, D_MODEL), lambda i, j: (i, 0)),
            pl.BlockSpec((1, D_MODEL), lambda i, j: (0, 0)),
            pl.BlockSpec((D_MODEL, GROUP_WIDTH), lambda i, j: (0, j)),
            tab_spec, tab_spec, tab_spec,
        ],
        out_specs=pl.BlockSpec((tm, GROUP_WIDTH), lambda i, j: (i, j)),
        out_shape=jax.ShapeDtypeStruct((rows, N_GROUPS * GROUP_WIDTH), BF16),
        scratch_shapes=[pltpu.VMEM((tm, D_MODEL), BF16)],
        compiler_params=_compiler_params(("arbitrary", "arbitrary")),
        name="in_proj",
    )(x2, g, w_f32, *tables)


def _diff_attn_kernel(lq1_ref, lk1_ref, lq2_ref, lk2_ref, g_ref, q_ref, k_ref, v_ref, o_ref, vt_ref, s_ref):
    seq = k_ref.shape[0]
    tq = ATTN_TQ
    n_tiles = seq // tq
    n_chunks = seq // ATTN_KC
    fold = ATTN_KC // SUBLANES
    assert n_tiles % 2 == 0 and n_tiles >= 4

    vt_ref[...] = v_ref[...].T
    lam = (jnp.exp(jnp.sum(lq1_ref[...] * lk1_ref[...])) - jnp.exp(jnp.sum(lq2_ref[...] * lk2_ref[...]))
           + LAMBDA_INIT)
    comp1_lane = lax.broadcasted_iota(jnp.int32, (tq, HEAD_DIM), 1) < DIFF_HALF

    def load_q_parts(t):
        q = q_ref[pl.ds(pl.multiple_of(t * tq, tq), tq), :]
        zero = jnp.zeros_like(q)
        return jnp.where(comp1_lane, q, zero), jnp.where(comp1_lane, zero, q)

    def score_chunk(slot, c, q_parts, m8):
        ks = k_ref[c * ATTN_KC:(c + 1) * ATTN_KC, :]
        new_m8 = []
        for comp in range(2):
            s = lax.dot_general(ks, q_parts[comp], (((1,), (1,)), ((), ())), preferred_element_type=F32)
            s_ref[slot, comp, c * ATTN_KC:(c + 1) * ATTN_KC, :] = s
            cm = jnp.max(s.reshape(fold, SUBLANES, tq), axis=0)
            new_m8.append(cm if m8 is None else jnp.maximum(m8[comp], cm))
        return new_m8

    def prob_chunk(slot, c, m, acc):
        vts = vt_ref[:, c * ATTN_KC:(c + 1) * ATTN_KC]
        new_acc = []
        for comp in range(2):
            l8, o = acc[comp]
            p = jnp.exp2(s_ref[slot, comp, c * ATTN_KC:(c + 1) * ATTN_KC, :] - m[comp])
            l8 = l8 + jnp.sum(p.reshape(fold, SUBLANES, tq), axis=0)
            o = o + jnp.dot(vts, p.astype(BF16), preferred_element_type=F32)
            new_acc.append((l8, o))
        return new_acc

    def col_max(m8):
        return [jnp.max(x, axis=0, keepdims=True) for x in m8]

    def zero_acc():
        return [(jnp.zeros((SUBLANES, tq), F32), jnp.zeros((HEAD_DIM, tq), F32)) for _ in range(2)]

    def finish(t, acc):
        outs = [o / jnp.sum(l8, axis=0, keepdims=True) for l8, o in acc]
        out = (outs[0] - lam * outs[1]).T
        y = out * _rms_scale(out) * g_ref[...]
        o_ref[pl.ds(pl.multiple_of(t * tq, tq), tq), :] = (y * (1.0 - LAMBDA_INIT)).astype(o_ref.dtype)

    def stage(t_next, t_cur, m_cur, t_prev, acc_prev):
        q_parts = load_q_parts(t_next[0]) if t_next is not None else None
        m8, acc = None, zero_acc()
        for c in range(n_chunks):
            if t_next is not None:
                m8 = score_chunk(t_next[1], c, q_parts, m8)
            if t_cur is not None:
                acc = prob_chunk(t_cur[1], c, m_cur, acc)
            if c == 0 and t_prev is not None:
                finish(t_prev, acc_prev)
        return (col_max(m8) if m8 is not None else None), acc

    m_cur, _ = stage((0, 0), None, None, None, None)
    m_cur, acc = stage((1, 1), (0, 0), m_cur, None, None)

    def pair(u, carry):
        m_odd, acc_even = carry
        m_even, acc_odd = stage((2 * u, 0), (2 * u - 1, 1), m_odd, 2 * u - 2, acc_even)
        m_odd, acc_even = stage((2 * u + 1, 1), (2 * u, 0), m_even, 2 * u - 1, acc_odd)
        return m_odd, acc_even

    m_cur, acc = lax.fori_loop(1, n_tiles // 2, pair, (m_cur, acc))
    _, acc_last = stage(None, (n_tiles - 1, 1), m_cur, n_tiles - 2, acc)
    finish(n_tiles - 1, acc_last)


def _diff_attn(proj, lq1, lk1, lq2, lk2, subln_g, batch, seq):
    lam_spec = pl.BlockSpec((1, DIFF_HALF), lambda b, h: (0, 0))
    head_blk = lambda col0: pl.BlockSpec((seq, HEAD_DIM), lambda b, h: (b, col0 + h))
    return pl.pallas_call(
        _diff_attn_kernel,
        grid=(batch, N_HEADS),
        in_specs=[
            lam_spec, lam_spec, lam_spec, lam_spec,
            pl.BlockSpec((1, HEAD_DIM), lambda b, h: (0, 0)),
            head_blk(GROUP_DQ * N_HEADS), head_blk(GROUP_DK * N_HEADS), head_blk(GROUP_DV * N_HEADS),
        ],
        out_specs=pl.BlockSpec((seq, HEAD_DIM), lambda b, h: (b, h)),
        out_shape=jax.ShapeDtypeStruct((batch * seq, GROUP_WIDTH), BF16),
        scratch_shapes=[pltpu.VMEM((HEAD_DIM, seq), BF16), pltpu.VMEM((2, 2, seq, ATTN_TQ), F32)],
        compiler_params=_compiler_params(("arbitrary", "arbitrary")),
        name="diff_attn",
    )(lq1, lk1, lq2, lk2, subln_g, proj, proj, proj)


N_REL_ROWS = 2 * WIN_H - 1
N_REL_COLS = 2 * WIN_W - 1
N_PAIR_TABLES = N_REL_ROWS - 1


def _na_kernel(rb_ref, q_ref, k_ref, v_ref, o_ref, tab_ref, *, batch):
    head = pl.program_id(0)
    rows = q_ref.shape[0] // (batch * GRID_W)
    band = WIN_H * GRID_W

    qc = lax.broadcasted_iota(jnp.int32, (GRID_W, LANES), 0)
    ln = lax.broadcasted_iota(jnp.int32, (GRID_W, LANES), 1)
    kc = ln & (GRID_W - 1)
    upper = ln >= GRID_W
    rel = jnp.clip(kc - qc + (WIN_W - 1), 0, N_REL_COLS - 1)
    col_start = jnp.clip(qc - WIN_W // 2, 0, GRID_W - WIN_W)
    inside = (kc >= col_start) & (kc < col_start + WIN_W)
    base = head * (N_REL_ROWS * N_REL_COLS)

    def build(t, carry):
        acc = jnp.zeros((GRID_W, LANES), F32)
        for d in range(N_REL_COLS):
            lo = rb_ref[base + t * N_REL_COLS + d]
            hi = rb_ref[base + (t + 1) * N_REL_COLS + d]
            acc = jnp.where(rel == d, jnp.where(upper, hi, lo), acc)
        tab_ref[t] = jnp.where(inside, acc * LOG2_E, NEG_INF)
        return carry

    lax.fori_loop(0, N_PAIR_TABLES, build, 0)

    def band_start(r):
        return min(max(r - WIN_H // 2, 0), rows - WIN_H)

    def token_rows(b, r, n_rows):
        start = (b * rows + r) * GRID_W
        return slice(start, start + n_rows * GRID_W)

    def score(b, r):
        q = q_ref[token_rows(b, r, 1), :]
        kb = k_ref[token_rows(b, band_start(r), WIN_H), :]
        return lax.dot_general(q, kb, (((1,), (1,)), ((), ())), preferred_element_type=F32)

    def softmax(r, s):
        rel0 = band_start(r) - r + (WIN_H - 1)
        parts = [s[:, jj * LANES:(jj + 1) * LANES] + tab_ref[rel0 + 2 * jj] for jj in range(band // LANES)]
        logits = jnp.concatenate(parts, axis=-1)
        p = jnp.exp2(logits - jnp.max(logits, axis=-1, keepdims=True))
        return p.astype(BF16), jnp.sum(p, axis=-1, keepdims=True)

    def value(b, r, p, l):
        vb = v_ref[token_rows(b, band_start(r), WIN_H), :]
        o_ref[token_rows(b, r, 1), :] = (jnp.dot(p, vb, preferred_element_type=F32) / l).astype(o_ref.dtype)

    units = [(b, r) for b in range(batch) for r in range(rows)]
    groups = [units[g:g + NA_ROWS_PER_GROUP] for g in range(0, len(units), NA_ROWS_PER_GROUP)]
    scores = {u: score(*u) for u in groups[0]}
    for gi, group in enumerate(groups):
        probs = {u: softmax(u[1], scores.pop(u)) for u in group}
        nxt = groups[gi + 1] if gi + 1 < len(groups) else ()
        for i, u in enumerate(group):
            if i < len(nxt):
                scores[nxt[i]] = score(*nxt[i])
            value(*u, *probs[u])


def _na_attn(proj, rel_bias_flat, batch, seq):
    blk = lambda group: pl.BlockSpec((batch * seq, HEAD_DIM), lambda h: (0, group * N_HEADS + h))
    return pl.pallas_call(
        functools.partial(_na_kernel, batch=batch),
        grid=(N_HEADS,),
        in_specs=[
            pl.BlockSpec(memory_space=pltpu.SMEM),
            blk(GROUP_NQ), blk(GROUP_NK), blk(GROUP_NV),
        ],
        out_specs=pl.BlockSpec((batch * seq, HEAD_DIM), lambda h: (0, h)),
        out_shape=jax.ShapeDtypeStruct((batch * seq, GROUP_WIDTH), BF16),
        scratch_shapes=[pltpu.VMEM((N_PAIR_TABLES, GRID_W, LANES), F32)],
        compiler_params=_compiler_params(("arbitrary",)),
        name="na_attn",
    )(rel_bias_flat, proj, proj, proj)


def _out_proj_kernel(x_ref, a_ref, n_ref, wa_ref, wn_ref, g_ref, x1_ref, h_ref, wa_bf_ref, wn_bf_ref):
    @pl.when(pl.program_id(0) == 0)
    def _():
        wa_bf_ref[...] = wa_ref[...].astype(BF16)
        wn_bf_ref[...] = wn_ref[...].astype(BF16)

    for mc in range(x_ref.shape[0] // OUT_MC):
        rows = slice(mc * OUT_MC, (mc + 1) * OUT_MC)
        acc = jnp.dot(a_ref[rows, :], wa_bf_ref[...], preferred_element_type=F32)
        acc = acc + jnp.dot(n_ref[rows, :], wn_bf_ref[...], preferred_element_type=F32)
        x1 = x_ref[rows, :] + acc
        x1_ref[rows, :] = x1
        h_ref[rows, :] = (x1 * _rms_scale(x1) * g_ref[...]).astype(BF16)


def _out_proj(x2, a_out, n_out, w_out_f32, g_mlp):
    rows = x2.shape[0]
    tm = OUT_TM
    row_blk = lambda width: pl.BlockSpec((tm, width), lambda i: (i, 0))
    w_half = lambda half: pl.BlockSpec((GROUP_WIDTH, D_MODEL), lambda i: (half, 0), pipeline_mode=pl.Buffered(1))
    return pl.pallas_call(
        _out_proj_kernel,
        grid=(rows // tm,),
        in_specs=[
            row_blk(D_MODEL), row_blk(GROUP_WIDTH), row_blk(GROUP_WIDTH),
            w_half(0), w_half(1),
            pl.BlockSpec((1, D_MODEL), lambda i: (0, 0)),
        ],
        out_specs=[row_blk(D_MODEL), row_blk(D_MODEL)],
        out_shape=[jax.ShapeDtypeStruct((rows, D_MODEL), F32), jax.ShapeDtypeStruct((rows, D_MODEL), BF16)],
        scratch_shapes=[pltpu.VMEM((GROUP_WIDTH, D_MODEL), BF16), pltpu.VMEM((GROUP_WIDTH, D_MODEL), BF16)],
        compiler_params=_compiler_params(("arbitrary",)),
        name="out_proj",
    )(x2, a_out, n_out, w_out_f32, w_out_f32, g_mlp)


def _mlp_kernel(h_ref, x1_ref, wu_ref, wd_ref, gf_ref, o_ref):
    j = pl.program_id(1)
    last_j = pl.num_programs(1) - 1
    tm = o_ref.shape[0]
    piece = x1_ref.shape[0]
    chunks = [(start, start + MLP_MC) for start in range(0, tm, MLP_MC)]
    n_chunks = len(chunks)

    def step(first, last):
        wu = wu_ref[...].astype(BF16)
        wd = wd_ref[...].astype(BF16)

        def up(c):
            u = jnp.dot(h_ref[slice(*chunks[c]), :], wu, preferred_element_type=F32)
            r = jnp.maximum(u, 0.0)
            return (r * r).astype(BF16)

        def down(c, act):
            rows = slice(*chunks[c])
            y = jnp.dot(act, wd, preferred_element_type=F32)
            if not first:
                y = y + o_ref[rows, :]
            if last:
                y = y * _rms_scale(y) * gf_ref[...]
            o_ref[rows, :] = y

        if last:
            o_ref[tm - piece:tm, :] += x1_ref[...]
        act = up(0)
        for c in range(n_chunks):
            nxt = up(c + 1) if c + 1 < n_chunks else None
            down(c, act)
            act = nxt
        if first:
            o_ref[0:piece, :] += x1_ref[...]
        elif not last:
            o_ref[pl.ds(pl.multiple_of(j * piece, piece), piece), :] += x1_ref[...]

    pl.when(j == 0)(functools.partial(step, True, False))
    pl.when((j > 0) & (j < last_j))(functools.partial(step, False, False))
    pl.when(j == last_j)(functools.partial(step, False, True))


def _mlp(h, x1, w_up_f32, w_down_f32, g_final):
    rows = x1.shape[0]
    tm, tf = MLP_TM, MLP_TF
    steps = D_FF // tf
    piece = tm // steps
    return pl.pallas_call(
        _mlp_kernel,
        grid=(rows // tm, steps),
        in_specs=[
            pl.BlockSpec((tm, D_MODEL), lambda i, j: (i, 0)),
            pl.BlockSpec((piece, D_MODEL), lambda i, j: (i * steps + j, 0)),
            pl.BlockSpec((D_MODEL, tf), lambda i, j: (0, j)),
            pl.BlockSpec((tf, D_MODEL), lambda i, j: (j, 0)),
            pl.BlockSpec((1, D_MODEL), lambda i, j: (0, 0)),
        ],
        out_specs=pl.BlockSpec((tm, D_MODEL), lambda i, j: (i, 0)),
        out_shape=jax.ShapeDtypeStruct((rows, D_MODEL), F32),
        compiler_params=_compiler_params(("arbitrary", "arbitrary")),
        name="mlp",
    )(h, x1, w_up_f32, w_down_f32, g_final)


def kernel(x, norm_mix_g, w_in, lambda_q1, lambda_k1, lambda_q2, lambda_k2, diff_subln_g, na_rel_bias, w_out,
           norm_mlp_g, w_up, w_down, norm_final_g):
    batch, seq, d_model = x.shape
    assert d_model == D_MODEL and w_in.shape == (1, D_MODEL, N_GROUPS * GROUP_WIDTH)
    assert seq % PROJ_TM == 0 and seq % ATTN_TQ == 0 and seq % ATTN_KC == 0 and seq % (GRID_W * WIN_H) == 0
    x2 = x.reshape(batch * seq, D_MODEL)

    proj = _in_proj(x2, norm_mix_g, w_in[0], seq)
    a_out = _diff_attn(proj, lambda_q1, lambda_k1, lambda_q2, lambda_k2, diff_subln_g, batch, seq)
    n_out = _na_attn(proj, na_rel_bias[0].reshape(-1), batch, seq)
    x1, h_mlp = _out_proj(x2, a_out, n_out, w_out[0], norm_mlp_g)
    y = _mlp(h_mlp, x1, w_up[0], w_down[0], norm_final_g.reshape(1, D_MODEL))
    return y.reshape(batch, seq, D_MODEL)
```

```python
import functools
import math

import jax
import jax.numpy as jnp
import numpy as np
from jax import lax
from jax.experimental import pallas as pl
from jax.experimental.pallas import tpu as pltpu

F32 = jnp.float32
BF16 = jnp.bfloat16

D_MODEL = 2048
HEAD_DIM = 128
N_HEADS = 8
GROUP_WIDTH = N_HEADS * HEAD_DIM
N_GROUPS = 6
GROUP_DQ, GROUP_DK, GROUP_DV, GROUP_NQ, GROUP_NK, GROUP_NV = range(N_GROUPS)
DIFF_HALF = HEAD_DIM // 2
ROT_DIM = DIFF_HALF // 4
ROT_HALF = ROT_DIM // 2
ROPE_THETA = 500000.0
GRID_W = 64
WIN_H = 8
WIN_W = 16
D_FF = 4 * D_MODEL
EPS = 1e-5
NEG_INF = -1e30
LAMBDA_INIT = 0.8 - 0.6 * math.exp(-0.3 * 0)

LANES = 128
SUBLANES = 8
VMEM_LIMIT_BYTES = 56 * 1024 * 1024

PROJ_TM = 1024
PROJ_MC = 512
ATTN_TQ = 256
ATTN_KC = 512
OUT_TM = 512
OUT_MC = 256
MLP_TM = 1024
MLP_TF = 512
MLP_MC = 1024
NA_ROWS_PER_GROUP = 16


def _compiler_params(semantics):
    return pltpu.CompilerParams(dimension_semantics=semantics, vmem_limit_bytes=VMEM_LIMIT_BYTES)


def _row_chunks(total, chunk):
    bounds = list(range(0, total, chunk)) + [total]
    bounds.insert(-1, bounds[-2] + chunk // 2)
    return list(zip(bounds[:-1], bounds[1:]))


def _rms_scale(x):
    return lax.rsqrt(jnp.mean(x * x, axis=-1, keepdims=True) + EPS)


LOG2_E = math.log2(math.e)
SCORE_SCALE_LOG2 = LOG2_E / math.sqrt(DIFF_HALF)
NA_SCALE_LOG2 = LOG2_E / math.sqrt(HEAD_DIM)


def _in_proj_kernel(x_ref, g_ref, w_ref, c_ref, s_lo_ref, s_hi_ref, o_ref, h_ref):
    j = pl.program_id(1)

    def step(first, rotary):
        if rotary:
            q_scale = SCORE_SCALE_LOG2 if first else 1.0
        else:
            q_scale = jnp.where(j == GROUP_NQ, NA_SCALE_LOG2, 1.0).astype(F32)
        w = w_ref[...].astype(BF16)
        for start, stop in _row_chunks(x_ref.shape[0], PROJ_MC):
            rows = slice(start, stop)
            if first:
                x = x_ref[rows, :]
                h_ref[rows, :] = (x * _rms_scale(x) * g_ref[...]).astype(BF16)
            acc = jnp.dot(h_ref[rows, :], w, preferred_element_type=F32)
            if not rotary:
                o_ref[rows, :] = (acc * q_scale).astype(BF16)
                continue
            c = c_ref[rows, :] * q_scale
            s_lo = s_lo_ref[rows, :] * q_scale
            s_hi = s_hi_ref[rows, :] * q_scale
            for hh in range(N_HEADS):
                a = acc[:, hh * HEAD_DIM:(hh + 1) * HEAD_DIM]
                r = a * c + pltpu.roll(a, LANES - ROT_HALF, 1) * s_lo + pltpu.roll(a, ROT_HALF, 1) * s_hi
                o_ref[rows, hh * HEAD_DIM:(hh + 1) * HEAD_DIM] = r.astype(BF16)

    assert GROUP_DQ == 0
    pl.when(j == GROUP_DQ)(functools.partial(step, True, True))
    pl.when(j == GROUP_DK)(functools.partial(step, False, True))
    pl.when((j != GROUP_DQ) & (j != GROUP_DK))(functools.partial(step, False, False))


@functools.lru_cache(maxsize=None)
def _rotary_lane_tables(seq):
    lane = np.arange(LANES)
    inv_freq = np.power(np.float32(ROPE_THETA), -np.arange(0, ROT_DIM, 2, dtype=np.float32) / np.float32(ROT_DIM))
    ang = (np.arange(seq, dtype=np.float32)[:, None] * inv_freq[lane % ROT_HALF][None, :]).astype(np.float32)
    in_lo = (lane % DIFF_HALF) < ROT_HALF
    in_hi = ((lane % DIFF_HALF) >= ROT_HALF) & ((lane % DIFF_HALF) < ROT_DIM)
    cos, sin = np.cos(ang).astype(np.float32), np.sin(ang).astype(np.float32)
    c = np.where(in_lo | in_hi, cos, np.float32(1.0))
    s_lo = np.where(in_lo, -sin, np.float32(0.0))
    s_hi = np.where(in_hi, sin, np.float32(0.0))
    return c, s_lo, s_hi


def _in_proj(x2, g, w_f32, seq):
    rows = x2.shape[0]
    tm = PROJ_TM
    tables = [jnp.asarray(t) for t in _rotary_lane_tables(seq)]
    pos_blocks = seq // tm
    tab_spec = pl.BlockSpec((tm, LANES), lambda i, j: (i % pos_blocks, 0))
    return pl.pallas_call(
        _in_proj_kernel,
        grid=(rows // tm, N_GROUPS),
        in_specs=[
            pl.BlockSpec((tm, D_MODEL), lambda i, j: (i, 0)),
            pl.BlockSpec((1, D_MODEL), lambda i, j: (0, 0)),
            pl.BlockSpec((D_MODEL, GROUP_WIDTH), lambda i, j: (0, j)),
            tab_spec, tab_spec, tab_spec,
        ],
        out_specs=pl.BlockSpec((tm, GROUP_WIDTH), lambda i, j: (i, j)),
        out_shape=jax.ShapeDtypeStruct((rows, N_GROUPS * GROUP_WIDTH), BF16),
        scratch_shapes=[pltpu.VMEM((tm, D_MODEL), BF16)],
        compiler_params=_compiler_params(("arbitrary", "arbitrary")),
        name="in_proj",
    )(x2, g, w_f32, *tables)


def _diff_attn_kernel(lq1_ref, lk1_ref, lq2_ref, lk2_ref, g_ref, q_ref, k_ref, v_ref, o_ref, vt_ref, s_ref):
    seq = k_ref.shape[0]
    tq = ATTN_TQ
    n_tiles = seq // tq
    n_chunks = seq // ATTN_KC
    fold = ATTN_KC // SUBLANES
    assert n_tiles % 2 == 0 and n_tiles >= 4

    vt_ref[...] = v_ref[...].T
    lam = (jnp.exp(jnp.sum(lq1_ref[...] * lk1_ref[...])) - jnp.exp(jnp.sum(lq2_ref[...] * lk2_ref[...]))
           + LAMBDA_INIT)
    comp1_lane = lax.broadcasted_iota(jnp.int32, (tq, HEAD_DIM), 1) < DIFF_HALF

    def load_q_parts(t):
        q = q_ref[pl.ds(pl.multiple_of(t * tq, tq), tq), :]
        zero = jnp.zeros_like(q)
        return jnp.where(comp1_lane, q, zero), jnp.where(comp1_lane, zero, q)

    def score_chunk(slot, c, q_parts, m8):
        ks = k_ref[c * ATTN_KC:(c + 1) * ATTN_KC, :]
        new_m8 = []
        for comp in range(2):
            s = lax.dot_general(ks, q_parts[comp], (((1,), (1,)), ((), ())), preferred_element_type=F32)
            s_ref[slot, comp, c * ATTN_KC:(c + 1) * ATTN_KC, :] = s
            cm = jnp.max(s.reshape(fold, SUBLANES, tq), axis=0)
            new_m8.append(cm if m8 is None else jnp.maximum(m8[comp], cm))
        return new_m8

    def prob_chunk(slot, c, m, acc):
        vts = vt_ref[:, c * ATTN_KC:(c + 1) * ATTN_KC]
        new_acc = []
        for comp in range(2):
            l8, o = acc[comp]
            p = jnp.exp2(s_ref[slot, comp, c * ATTN_KC:(c + 1) * ATTN_KC, :] - m[comp])
            l8 = l8 + jnp.sum(p.reshape(fold, SUBLANES, tq), axis=0)
            o = o + jnp.dot(vts, p.astype(BF16), preferred_element_type=F32)
            new_acc.append((l8, o))
        return new_acc

    def col_max(m8):
        return [jnp.max(x, axis=0, keepdims=True) for x in m8]

    def zero_acc():
        return [(jnp.zeros((SUBLANES, tq), F32), jnp.zeros((HEAD_DIM, tq), F32)) for _ in range(2)]

    def finish(t, acc):
        outs = [o / jnp.sum(l8, axis=0, keepdims=True) for l8, o in acc]
        out = (outs[0] - lam * outs[1]).T
        y = out * _rms_scale(out) * g_ref[...]
        o_ref[pl.ds(pl.multiple_of(t * tq, tq), tq), :] = (y * (1.0 - LAMBDA_INIT)).astype(o_ref.dtype)

    def stage(t_next, t_cur, m_cur, t_prev, acc_prev):
        q_parts = load_q_parts(t_next[0]) if t_next is not None else None
        m8, acc = None, zero_acc()
        for c in range(n_chunks):
            if t_next is not None:
                m8 = score_chunk(t_next[1], c, q_parts, m8)
            if t_cur is not None:
                acc = prob_chunk(t_cur[1], c, m_cur, acc)
            if c == 0 and t_prev is not None:
                finish(t_prev, acc_prev)
        return (col_max(m8) if m8 is not None else None), acc

    m_cur, _ = stage((0, 0), None, None, None, None)
    m_cur, acc = stage((1, 1), (0, 0), m_cur, None, None)

    def pair(u, carry):
        m_odd, acc_even = carry
        m_even, acc_odd = stage((2 * u, 0), (2 * u - 1, 1), m_odd, 2 * u - 2, acc_even)
        m_odd, acc_even = stage((2 * u + 1, 1), (2 * u, 0), m_even, 2 * u - 1, acc_odd)
        return m_odd, acc_even

    m_cur, acc = lax.fori_loop(1, n_tiles // 2, pair, (m_cur, acc))
    _, acc_last = stage(None, (n_tiles - 1, 1), m_cur, n_tiles - 2, acc)
    finish(n_tiles - 1, acc_last)


def _diff_attn(proj, lq1, lk1, lq2, lk2, subln_g, batch, seq):
    lam_spec = pl.BlockSpec((1, DIFF_HALF), lambda b, h: (0, 0))
    head_blk = lambda col0: pl.BlockSpec((seq, HEAD_DIM), lambda b, h: (b, col0 + h))
    return pl.pallas_call(
        _diff_attn_kernel,
        grid=(batch, N_HEADS),
        in_specs=[
            lam_spec, lam_spec, lam_spec, lam_spec,
            pl.BlockSpec((1, HEAD_DIM), lambda b, h: (0, 0)),
            head_blk(GROUP_DQ * N_HEADS), head_blk(GROUP_DK * N_HEADS), head_blk(GROUP_DV * N_HEADS),
        ],
        out_specs=pl.BlockSpec((seq, HEAD_DIM), lambda b, h: (b, h)),
        out_shape=jax.ShapeDtypeStruct((batch * seq, GROUP_WIDTH), BF16),
        scratch_shapes=[pltpu.VMEM((HEAD_DIM, seq), BF16), pltpu.VMEM((2, 2, seq, ATTN_TQ), F32)],
        compiler_params=_compiler_params(("arbitrary", "arbitrary")),
        name="diff_attn",
    )(lq1, lk1, lq2, lk2, subln_g, proj, proj, proj)


N_REL_ROWS = 2 * WIN_H - 1
N_REL_COLS = 2 * WIN_W - 1
N_PAIR_TABLES = N_REL_ROWS - 1


def _na_kernel(rb_ref, q_ref, k_ref, v_ref, o_ref, tab_ref, *, batch):
    head = pl.program_id(0)
    rows = q_ref.shape[0] // (batch * GRID_W)
    band = WIN_H * GRID_W

    qc = lax.broadcasted_iota(jnp.int32, (GRID_W, LANES), 0)
    ln = lax.broadcasted_iota(jnp.int32, (GRID_W, LANES), 1)
    kc = ln & (GRID_W - 1)
    upper = ln >= GRID_W
    rel = jnp.clip(kc - qc + (WIN_W - 1), 0, N_REL_COLS - 1)
    col_start = jnp.clip(qc - WIN_W // 2, 0, GRID_W - WIN_W)
    inside = (kc >= col_start) & (kc < col_start + WIN_W)
    base = head * (N_REL_ROWS * N_REL_COLS)

    def build(t, carry):
        acc = jnp.zeros((GRID_W, LANES), F32)
        for d in range(N_REL_COLS):
            lo = rb_ref[base + t * N_REL_COLS + d]
            hi = rb_ref[base + (t + 1) * N_REL_COLS + d]
            acc = jnp.where(rel == d, jnp.where(upper, hi, lo), acc)
        tab_ref[t] = jnp.where(inside, acc * LOG2_E, NEG_INF)
        return carry

    lax.fori_loop(0, N_PAIR_TABLES, build, 0)

    def band_start(r):
        return min(max(r - WIN_H // 2, 0), rows - WIN_H)

    def token_rows(b, r, n_rows):
        start = (b * rows + r) * GRID_W
        return slice(start, start + n_rows * GRID_W)

    def score(b, r):
        q = q_ref[token_rows(b, r, 1), :]
        kb = k_ref[token_rows(b, band_start(r), WIN_H), :]
        return lax.dot_general(q, kb, (((1,), (1,)), ((), ())), preferred_element_type=F32)

    def softmax(r, s):
        rel0 = band_start(r) - r + (WIN_H - 1)
        parts = [s[:, jj * LANES:(jj + 1) * LANES] + tab_ref[rel0 + 2 * jj] for jj in range(band // LANES)]
        logits = jnp.concatenate(parts, axis=-1)
        p = jnp.exp2(logits - jnp.max(logits, axis=-1, keepdims=True))
        return p.astype(BF16), jnp.sum(p, axis=-1, keepdims=True)

    def value(b, r, p, l):
        vb = v_ref[token_rows(b, band_start(r), WIN_H), :]
        o_ref[token_rows(b, r, 1), :] = (jnp.dot(p, vb, preferred_element_type=F32) / l).astype(o_ref.dtype)

    units = [(b, r) for b in range(batch) for r in range(rows)]
    groups = [units[g:g + NA_ROWS_PER_GROUP] for g in range(0, len(units), NA_ROWS_PER_GROUP)]
    scores = {u: score(*u) for u in groups[0]}
    for gi, group in enumerate(groups):
        probs = {u: softmax(u[1], scores.pop(u)) for u in group}
        nxt = groups[gi + 1] if gi + 1 < len(groups) else ()
        for i, u in enumerate(group):
            if i < len(nxt):
                scores[nxt[i]] = score(*nxt[i])
            value(*u, *probs[u])


def _na_attn(proj, rel_bias_flat, batch, seq):
    blk = lambda group: pl.BlockSpec((batch * seq, HEAD_DIM), lambda h: (0, group * N_HEADS + h))
    return pl.pallas_call(
        functools.partial(_na_kernel, batch=batch),
        grid=(N_HEADS,),
        in_specs=[
            pl.BlockSpec(memory_space=pltpu.SMEM),
            blk(GROUP_NQ), blk(GROUP_NK), blk(GROUP_NV),
        ],
        out_specs=pl.BlockSpec((batch * seq, HEAD_DIM), lambda h: (0, h)),
        out_shape=jax.ShapeDtypeStruct((batch * seq, GROUP_WIDTH), BF16),
        scratch_shapes=[pltpu.VMEM((N_PAIR_TABLES, GRID_W, LANES), F32)],
        compiler_params=_compiler_params(("arbitrary",)),
        name="na_attn",
    )(rel_bias_flat, proj, proj, proj)


def _out_proj_kernel(x_ref, a_ref, n_ref, wa_ref, wn_ref, g_ref, x1_ref, h_ref, wa_bf_ref, wn_bf_ref):
    @pl.when(pl.program_id(0) == 0)
    def _():
        wa_bf_ref[...] = wa_ref[...].astype(BF16)
        wn_bf_ref[...] = wn_ref[...].astype(BF16)

    for mc in range(x_ref.shape[0] // OUT_MC):
        rows = slice(mc * OUT_MC, (mc + 1) * OUT_MC)
        acc = jnp.dot(a_ref[rows, :], wa_bf_ref[...], preferred_element_type=F32)
        acc = acc + jnp.dot(n_ref[rows, :], wn_bf_ref[...], preferred_element_type=F32)
        x1 = x_ref[rows, :] + acc
        x1_ref[rows, :] = x1
        h_ref[rows, :] = (x1 * _rms_scale(x1) * g_ref[...]).astype(BF16)


def _out_proj(x2, a_out, n_out, w_out_f32, g_mlp):
    rows = x2.shape[0]
    tm = OUT_TM
    row_blk = lambda width: pl.BlockSpec((tm, width), lambda i: (i, 0))
    w_half = lambda half: pl.BlockSpec((GROUP_WIDTH, D_MODEL), lambda i: (half, 0), pipeline_mode=pl.Buffered(1))
    return pl.pallas_call(
        _out_proj_kernel,
        grid=(rows // tm,),
        in_specs=[
            row_blk(D_MODEL), row_blk(GROUP_WIDTH), row_blk(GROUP_WIDTH),
            w_half(0), w_half(1),
            pl.BlockSpec((1, D_MODEL), lambda i: (0, 0)),
        ],
        out_specs=[row_blk(D_MODEL), row_blk(D_MODEL)],
        out_shape=[jax.ShapeDtypeStruct((rows, D_MODEL), F32), jax.ShapeDtypeStruct((rows, D_MODEL), BF16)],
        scratch_shapes=[pltpu.VMEM((GROUP_WIDTH, D_MODEL), BF16), pltpu.VMEM((GROUP_WIDTH, D_MODEL), BF16)],
        compiler_params=_compiler_params(("arbitrary",)),
        name="out_proj",
    )(x2, a_out, n_out, w_out_f32, w_out_f32, g_mlp)


def _mlp_kernel(h_ref, x1_ref, wu_ref, wd_ref, gf_ref, o_ref):
    j = pl.program_id(1)
    last_j = pl.num_programs(1) - 1
    tm = o_ref.shape[0]
    piece = x1_ref.shape[0]
    chunks = [(start, start + MLP_MC) for start in range(0, tm, MLP_MC)]
    n_chunks = len(chunks)

    def step(first, last):
        wu = wu_ref[...].astype(BF16)
        wd = wd_ref[...].astype(BF16)

        def up(c):
            u = jnp.dot(h_ref[slice(*chunks[c]), :], wu, preferred_element_type=F32)
            r = jnp.maximum(u, 0.0)
            return (r * r).astype(BF16)

        def down(c, act):
            rows = slice(*chunks[c])
            y = jnp.dot(act, wd, preferred_element_type=F32)
            if not first:
                y = y + o_ref[rows, :]
            if last:
                y = y * _rms_scale(y) * gf_ref[...]
            o_ref[rows, :] = y

        if last:
            o_ref[tm - piece:tm, :] += x1_ref[...]
        act = up(0)
        for c in range(n_chunks):
            nxt = up(c + 1) if c + 1 < n_chunks else None
            down(c, act)
            act = nxt
        if first:
            o_ref[0:piece, :] += x1_ref[...]
        elif not last:
            o_ref[pl.ds(pl.multiple_of(j * piece, piece), piece), :] += x1_ref[...]

    pl.when(j == 0)(functools.partial(step, True, False))
    pl.when((j > 0) & (j < last_j))(functools.partial(step, False, False))
    pl.when(j == last_j)(functools.partial(step, False, True))


def _mlp(h, x1, w_up_f32, w_down_f32, g_final):
    rows = x1.shape[0]
    tm, tf = MLP_TM, MLP_TF
    steps = D_FF // tf
    piece = tm // steps
    return pl.pallas_call(
        _mlp_kernel,
        grid=(rows // tm, steps),
        in_specs=[
            pl.BlockSpec((tm, D_MODEL), lambda i, j: (i, 0)),
            pl.BlockSpec((piece, D_MODEL), lambda i, j: (i * steps + j, 0)),
            pl.BlockSpec((D_MODEL, tf), lambda i, j: (0, j)),
            pl.BlockSpec((tf, D_MODEL), lambda i, j: (j, 0)),
            pl.BlockSpec((1, D_MODEL), lambda i, j: (0, 0)),
        ],
        out_specs=pl.BlockSpec((tm, D_MODEL), lambda i, j: (i, 0)),
        out_shape=jax.ShapeDtypeStruct((rows, D_MODEL), F32),
        compiler_params=_compiler_params(("arbitrary", "arbitrary")),
        name="mlp",
    )(h, x1, w_up_f32, w_down_f32, g_final)


def kernel(x, norm_mix_g, w_in, lambda_q1, lambda_k1, lambda_q2, lambda_k2, diff_subln_g, na_rel_bias, w_out,
           norm_mlp_g, w_up, w_down, norm_final_g):
    batch, seq, d_model = x.shape
    assert d_model == D_MODEL and w_in.shape == (1, D_MODEL, N_GROUPS * GROUP_WIDTH)
    assert seq % PROJ_TM == 0 and seq % ATTN_TQ == 0 and seq % ATTN_KC == 0 and seq % (GRID_W * WIN_H) == 0
    x2 = x.reshape(batch * seq, D_MODEL)

    proj = _in_proj(x2, norm_mix_g, w_in[0], seq)
    a_out = _diff_attn(proj, lambda_q1, lambda_k1, lambda_q2, lambda_k2, diff_subln_g, batch, seq)
    n_out = _na_attn(proj, na_rel_bias[0].reshape(-1), batch, seq)
    x1, h_mlp = _out_proj(x2, a_out, n_out, w_out[0], norm_mlp_g)
    y = _mlp(h_mlp, x1, w_up[0], w_down[0], norm_final_g.reshape(1, D_MODEL))
    return y.reshape(batch, seq, D_MODEL)
```
